```python
import math
import jax, jax.numpy as jnp
from jax import lax
import numpy as np

D_MODEL = 1024
BATCH = 4
SEQ = 4096
DEPTH = 4
DEC_BATCH = 32
DEC_SEQ = 4
PAST_LEN = 8192
PAGE_SIZE = 128

N_A = DEPTH // 2
N_B = DEPTH - N_A
D_INNER = 2 * D_MODEL
SSM_HEAD_DIM = 64
SSM_HEADS = D_INNER // SSM_HEAD_DIM
SSM_GROUPS = 8
HEADS_PER_GROUP = SSM_HEADS // SSM_GROUPS
D_STATE = 128
CONV_W = 4
GN = SSM_GROUPS * D_STATE
CONV_DIM = D_INNER + 2 * GN
IN_DIM = D_INNER + CONV_DIM + SSM_HEADS
CHUNK = 128
ATT_HEAD_DIM = 64
ATT_HEADS = D_MODEL // (2 * ATT_HEAD_DIM)
Q_DIM = ATT_HEADS * 2 * ATT_HEAD_DIM
K_DIM = Q_DIM
V_HEAD_DIM = 2 * ATT_HEAD_DIM
V_DIM = ATT_HEADS * V_HEAD_DIM
Q_BLOCK = 128
D_FF = 4 * D_MODEL
EPS = 1e-6
F32 = jnp.float32

kernel_name = "yoco_mamba2_diffattn_decoder_step"


def rms_norm(x, g):
    xf = x.astype(F32)
    y = xf * lax.rsqrt(jnp.mean(xf * xf, axis=-1, keepdims=True) + EPS)
    return (y * g.astype(F32)).astype(x.dtype)


def sq_relu_mlp(h, w_up, w_down):
    return jnp.square(jax.nn.relu(h @ w_up)) @ w_down


def causal_depthwise_conv(xbc, buf, w, b):
    L = xbc.shape[1]
    xpad = jnp.concatenate([buf.astype(xbc.dtype), xbc], axis=1)
    out = b
    for k in range(CONV_W):
        out = out + w[k] * xpad[:, k:k + L]
    return out, xpad[:, L:]


def gated_group_rmsnorm(y, z, g):
    shp = y.shape
    yz = (y.astype(F32) * jax.nn.silu(z.astype(F32))).reshape(shp[:-1] + (SSM_GROUPS, D_INNER // SSM_GROUPS))
    yz = yz * lax.rsqrt(jnp.mean(yz * yz, axis=-1, keepdims=True) + EPS)
    return yz.reshape(shp) * g.astype(F32)


def ssd_chunked(x, a, bm, cm, h0):
    b, L = x.shape[:2]
    q = min(CHUNK, L)
    c = L // q
    x = x.reshape(b, c, q, SSM_GROUPS, HEADS_PER_GROUP, SSM_HEAD_DIM)
    a = a.reshape(b, c, q, SSM_GROUPS, HEADS_PER_GROUP)
    bm = bm.reshape(b, c, q, SSM_GROUPS, D_STATE)
    cm = cm.reshape(b, c, q, SSM_GROUPS, D_STATE)
    a_cs = jnp.cumsum(a, axis=2)
    causal = jnp.tril(jnp.ones((q, q), dtype=bool))[None, None, :, :, None, None]
    seg = a_cs[:, :, :, None] - a_cs[:, :, None, :]
    decay = jnp.exp(jnp.where(causal, seg, -jnp.inf))
    cb = jnp.einsum('bclgn,bcsgn->bclsg', cm, bm)
    y_diag = jnp.einsum('bclsg,bclsgr,bcsgrp->bclgrp', cb, decay, x)
    decay_to_end = jnp.exp(a_cs[:, :, -1:] - a_cs)
    states = jnp.einsum('bclgn,bclgr,bclgrp->bcgrpn', bm, decay_to_end, x)
    chunk_decay = jnp.exp(a_cs[:, :, -1])
    h_init = h0.reshape(b, SSM_GROUPS, HEADS_PER_GROUP, SSM_HEAD_DIM, D_STATE)

    def step(h, inp):
        st, dec = inp
        return h * dec[..., None, None] + st, h

    h_last, h_in = lax.scan(step, h_init, (jnp.swapaxes(states, 0, 1), jnp.swapaxes(chunk_decay, 0, 1)))
    h_in = jnp.swapaxes(h_in, 0, 1)
    y_off = jnp.einsum('bclgn,bcgrpn,bclgr->bclgrp', cm, h_in, jnp.exp(a_cs))
    y = (y_diag + y_off).reshape(b, L, SSM_HEADS, SSM_HEAD_DIM)
    return y, h_last.reshape(b, SSM_HEADS, SSM_HEAD_DIM, D_STATE)


def mamba2_mixer(u, conv_buf, h0, w_in, conv_w, conv_b, dt_bias, a_log, d_skip, gnorm_w, w_out):
    b, L, _ = u.shape
    zxbcdt = u @ w_in
    z = zxbcdt[..., :D_INNER]
    xbc = zxbcdt[..., D_INNER:D_INNER + CONV_DIM]
    dt = zxbcdt[..., D_INNER + CONV_DIM:]
    xbc, new_buf = causal_depthwise_conv(xbc, conv_buf, conv_w, conv_b)
    xbc = jax.nn.silu(xbc)
    xs = xbc[..., :D_INNER].reshape(b, L, SSM_HEADS, SSM_HEAD_DIM).astype(F32)
    bm = xbc[..., D_INNER:D_INNER + GN].reshape(b, L, SSM_GROUPS, D_STATE).astype(F32)
    cm = xbc[..., D_INNER + GN:].reshape(b, L, SSM_GROUPS, D_STATE).astype(F32)
    dt = jax.nn.softplus(dt.astype(F32) + dt_bias.astype(F32))
    a = -jnp.exp(a_log.astype(F32))
    y, h_last = ssd_chunked(xs * dt[..., None], dt * a, bm, cm, h0.astype(F32))
    y = y + d_skip.astype(F32)[:, None] * xs
    y = gated_group_rmsnorm(y.reshape(b, L, D_INNER), z, gnorm_w)
    return y.astype(u.dtype) @ w_out, new_buf, h_last.astype(h0.dtype)


def shared_kv(x, kv_norm, w_kv, k_norm):
    b, t, _ = x.shape
    kv = rms_norm(x, kv_norm) @ w_kv
    k = rms_norm(kv[..., :K_DIM].reshape(b, t, ATT_HEADS, 2, ATT_HEAD_DIM), k_norm)
    v = kv[..., K_DIM:].reshape(b, t, ATT_HEADS, V_HEAD_DIM)
    return k, v


def diff_attention(q, k, v, q_pos, k_pos, lam):
    s = jnp.einsum('bqhcd,bkhcd->bhcqk', q, k).astype(F32) * (ATT_HEAD_DIM ** -0.5)
    mask = k_pos[None, :] <= q_pos[:, None]
    p = jax.nn.softmax(jnp.where(mask, s, -jnp.inf), axis=-1)
    a = p[:, :, 0] - lam * p[:, :, 1]
    return jnp.einsum('bhqk,bkhe->bqhe', a.astype(v.dtype), v)


def causal_block_attention(q, k, v, lam):
    b, s = q.shape[:2]
    nb = s // Q_BLOCK
    qb = jnp.swapaxes(q.reshape(b, nb, Q_BLOCK, ATT_HEADS, 2, ATT_HEAD_DIM), 0, 1)
    k_pos = jnp.arange(s)

    def one(args):
        q_blk, i = args
        q_pos = i * Q_BLOCK + jnp.arange(Q_BLOCK)
        return diff_attention(q_blk, k, v, q_pos, k_pos, lam)

    o = lax.map(one, (qb, jnp.arange(nb)))
    return jnp.swapaxes(o, 0, 1).reshape(b, s, ATT_HEADS, V_HEAD_DIM)


def diff_attn_layer(x, attend, layer_idx, norm_g, w_q, q_norm, lq1, lk1, lq2, lk2, subln, w_o):
    b, t, _ = x.shape
    q = (rms_norm(x, norm_g) @ w_q).reshape(b, t, ATT_HEADS, 2, ATT_HEAD_DIM)
    q = rms_norm(q, q_norm)
    lam_init = 0.8 - 0.6 * math.exp(-0.3 * layer_idx)
    lam = (jnp.exp(jnp.sum(lq1.astype(F32) * lk1.astype(F32)))
           - jnp.exp(jnp.sum(lq2.astype(F32) * lk2.astype(F32))) + lam_init)
    o = attend(q, lam)
    o = rms_norm(o, subln) * (1.0 - lam_init)
    return o.reshape(b, t, V_DIM) @ w_o


def setup_inputs(seed: int = 0) -> dict:
    key = jax.random.key(seed)
    ks = jax.random.split(key, 40)

    def nrm(i, shape, scale):
        return jax.random.normal(ks[i], shape, F32) * scale

    n_pages = PAST_LEN // PAGE_SIZE
    n_pool = (5 * DEC_BATCH * n_pages + 3) // 4
    page_table = jax.random.permutation(ks[7], n_pool)[:DEC_BATCH * n_pages].reshape(DEC_BATCH, n_pages).astype(jnp.int32)
    dt0 = jnp.exp(jax.random.uniform(ks[12], (N_A, SSM_HEADS), F32, math.log(1e-3), math.log(1e-1)))
    dt0 = jnp.maximum(dt0, 1e-4)
    dt_bias = dt0 + jnp.log(-jnp.expm1(-dt0))
    a_log = jnp.log(jax.random.uniform(ks[13], (N_A, SSM_HEADS), F32, 1.0, 16.0))
    return {
        "x_prompt": nrm(0, (BATCH, SEQ, D_MODEL), 1.0),
        "x_sample": nrm(1, (DEC_BATCH, DEC_SEQ, D_MODEL), 1.0),
        "state_ssm": nrm(2, (N_A, DEC_BATCH, SSM_HEADS, SSM_HEAD_DIM, D_STATE), 0.1),
        "state_conv": nrm(3, (N_A, DEC_BATCH, CONV_W - 1, CONV_DIM), 1.0),
        "cache_k": nrm(4, (n_pool, PAGE_SIZE, ATT_HEADS, 2, ATT_HEAD_DIM), 1.0),
        "cache_v": nrm(5, (n_pool, PAGE_SIZE, ATT_HEADS, V_HEAD_DIM), 1.0),
        "page_table": page_table,
        "norm_a": 1.0 + nrm(8, (N_A, D_MODEL), 0.02),
        "w_in_a": nrm(9, (N_A, D_MODEL, IN_DIM), D_MODEL ** -0.5),
        "conv_w": nrm(10, (N_A, CONV_W, CONV_DIM), CONV_W ** -0.5),
        "conv_b": nrm(11, (N_A, CONV_DIM), 0.02),
        "dt_bias": dt_bias,
        "a_log": a_log,
        "d_skip": 1.0 + nrm(14, (N_A, SSM_HEADS), 0.02),
        "gnorm_w": 1.0 + nrm(15, (N_A, D_INNER), 0.02),
        "w_out_a": nrm(16, (N_A, D_INNER, D_MODEL), D_INNER ** -0.5),
        "kv_norm": 1.0 + nrm(17, (D_MODEL,), 0.02),
        "w_kv": nrm(18, (D_MODEL, K_DIM + V_DIM), D_MODEL ** -0.5),
        "k_norm": 1.0 + nrm(19, (ATT_HEAD_DIM,), 0.02),
        "norm_b": 1.0 + nrm(20, (N_B, D_MODEL), 0.02),
        "w_q": nrm(21, (N_B, D_MODEL, Q_DIM), D_MODEL ** -0.5),
        "q_norm": 1.0 + nrm(22, (N_B, ATT_HEAD_DIM), 0.02),
        "lam_q1": nrm(23, (N_B, ATT_HEAD_DIM), 0.1),
        "lam_k1": nrm(24, (N_B, ATT_HEAD_DIM), 0.1),
        "lam_q2": nrm(25, (N_B, ATT_HEAD_DIM), 0.1),
        "lam_k2": nrm(26, (N_B, ATT_HEAD_DIM), 0.1),
        "subln": 1.0 + nrm(27, (N_B, V_HEAD_DIM), 0.02),
        "w_o": nrm(28, (N_B, V_DIM, D_MODEL), V_DIM ** -0.5),
        "norm_m": 1.0 + nrm(29, (DEPTH, D_MODEL), 0.02),
        "w_up": nrm(30, (DEPTH, D_MODEL, D_FF), D_MODEL ** -0.5),
        "w_down": nrm(31, (DEPTH, D_FF, D_MODEL), D_FF ** -0.5),
    }


def reference(x_prompt, x_sample, state_ssm, state_conv, cache_k, cache_v, page_table,
              norm_a, w_in_a, conv_w, conv_b, dt_bias, a_log, d_skip, gnorm_w, w_out_a,
              kv_norm, w_kv, k_norm, norm_b, w_q, q_norm, lam_q1, lam_k1, lam_q2, lam_k2,
              subln, w_o, norm_m, w_up, w_down):

    def trunk(x, conv0, ssm0, make_attend):
        convs, ssms = [], []
        attend, k_new, v_new = None, None, None
        for l in range(DEPTH):
            if l < N_A:
                y, cbuf, hs = mamba2_mixer(rms_norm(x, norm_a[l]), conv0[l], ssm0[l], w_in_a[l], conv_w[l],
                                           conv_b[l], dt_bias[l], a_log[l], d_skip[l], gnorm_w[l], w_out_a[l])
                convs.append(cbuf)
                ssms.append(hs)
                x = x + y
            else:
                if l == N_A:
                    k_new, v_new = shared_kv(x, kv_norm, w_kv, k_norm)
                    attend = make_attend(k_new, v_new)
                j = l - N_A
                x = x + diff_attn_layer(x, attend, l, norm_b[j], w_q[j], q_norm[j], lam_q1[j], lam_k1[j],
                                        lam_q2[j], lam_k2[j], subln[j], w_o[j])
            x = x + sq_relu_mlp(rms_norm(x, norm_m[l]), w_up[l], w_down[l])
        return x, jnp.stack(ssms), jnp.stack(convs), k_new, v_new

    bp = x_prompt.shape[0]
    conv0_p = jnp.zeros((N_A, bp, CONV_W - 1, CONV_DIM), x_prompt.dtype)
    ssm0_p = jnp.zeros((N_A, bp, SSM_HEADS, SSM_HEAD_DIM, D_STATE), state_ssm.dtype)

    def prompt_attend_factory(k, v):
        return lambda q, lam: causal_block_attention(q, k, v, lam)

    y_p, ssm_p, conv_p, k_p, v_p = trunk(x_prompt, conv0_p, ssm0_p, prompt_attend_factory)

    bd = x_sample.shape[0]
    n_past = page_table.shape[1] * PAGE_SIZE
    k_past = cache_k[page_table].reshape(bd, n_past, ATT_HEADS, 2, ATT_HEAD_DIM)
    v_past = cache_v[page_table].reshape(bd, n_past, ATT_HEADS, V_HEAD_DIM)

    def sample_attend_factory(k, v):
        t = k.shape[1]
        k_all = jnp.concatenate([k_past.astype(k.dtype), k], axis=1)
        v_all = jnp.concatenate([v_past.astype(v.dtype), v], axis=1)
        q_pos = n_past + jnp.arange(t)
        k_pos = jnp.arange(n_past + t)
        return lambda q, lam: diff_attention(q, k_all, v_all, q_pos, k_pos, lam)

    y_s, ssm_s, conv_s, k_s, v_s = trunk(x_sample, state_conv, state_ssm, sample_attend_factory)

    return (y_p, y_s, ssm_p, conv_p, k_p, v_p, ssm_s, conv_s, k_s, v_s)
```

```python
import functools
import math

import jax
import jax.numpy as jnp
from jax import lax
from jax.experimental import pallas as pl
from jax.experimental.pallas import tpu as pltpu

F32 = jnp.float32
BF16 = jnp.bfloat16
EPS = 1e-6
LANES = 128
CHUNK = 128
CONV_W = 4
VMEM_LIMIT = 48 * 1024 * 1024
NEG = -1e30
SDS = jax.ShapeDtypeStruct


def _cparams(sem):
    return pltpu.CompilerParams(dimension_semantics=sem, vmem_limit_bytes=VMEM_LIMIT)


def _rms(x, g):
    ms = jnp.mean(x * x, axis=-1, keepdims=True)
    return x * lax.rsqrt(ms + EPS) * g


def _silu(x):
    return x * (1.0 / (1.0 + jnp.exp(-x)))


def _softplus(x):
    return jnp.maximum(x, 0.0) + jnp.log1p(jnp.exp(-jnp.abs(x)))


def _dot(a, b):
    return jnp.dot(a, b, preferred_element_type=F32)


def _dot_nt(a, b):
    return lax.dot_general(a, b, (((1,), (1,)), ((), ())), preferred_element_type=F32)


def _split_bf16(x):
    hi = x.astype(BF16)
    lo = (x - hi.astype(F32)).astype(BF16)
    return hi, lo


def _group_norm64(x, gmat):
    parts = []
    for c in range(x.shape[1] // 256):
        blk = x[:, c * 256:(c + 1) * 256]
        ms = _dot((blk * blk).astype(BF16), gmat)
        parts.append(blk * lax.rsqrt(ms + EPS))
    return jnp.concatenate(parts, axis=1)


def _inproj_body(x_ref, g_ref, w_ref, wdt_ref, o_ref, dt_ref, u_scr):
    @pl.when(pl.program_id(1) == 0)
    def _():
        u = _rms(x_ref[...], g_ref[...]).astype(BF16)
        u_scr[...] = u
        dt_ref[...] = _dot(u, wdt_ref[...])

    o_ref[...] = _dot(u_scr[...], w_ref[...])


def _inproj(x2, g, w_main, w_dt, tm, tn):
    m, d = x2.shape
    n = w_main.shape[1]
    return pl.pallas_call(
        _inproj_body,
        grid=(m // tm, n // tn),
        in_specs=[
            pl.BlockSpec((tm, d), lambda i, j: (i, 0)),
            pl.BlockSpec((1, d), lambda i, j: (0, 0)),
            pl.BlockSpec((d, tn), lambda i, j: (0, j)),
            pl.BlockSpec((d, LANES), lambda i, j: (0, 0)),
        ],
        out_specs=[
            pl.BlockSpec((tm, tn), lambda i, j: (i, j)),
            pl.BlockSpec((tm, LANES), lambda i, j: (i, 0)),
        ],
        out_shape=[SDS((m, n), F32), SDS((m, LANES), F32)],
        scratch_shapes=[pltpu.VMEM((tm, d), BF16)],
        compiler_params=_cparams(("parallel", "arbitrary")),
        name="inproj",
    )(x2, g, w_main, w_dt)


def _ssd_body(xa_ref, xb_ref, dt_ref, conv0_ref, h0_ref, cw_ref, cb_ref, dtb_ref, alog_ref,
              dsk_ref, e_ref, y_ref, hn_ref, convn_ref, xpad, dtpad, hst,
              *, ts, nh, hd, ng, ds):
    q = CHUNK
    di = nh * hd
    gn = ng * ds
    hpg = nh // ng
    gw = hpg * hd
    s = pl.program_id(1)

    @pl.when(s == 0)
    def _():
        xpad[...] = jnp.zeros_like(xpad)
        dtpad[...] = jnp.zeros_like(dtpad)
        xpad[8 - (CONV_W - 1):8, :] = conv0_ref[0]
        hst[...] = h0_ref[0]

    xpad[8:8 + ts, 0:di] = xa_ref[0]
    xpad[8:8 + ts, di:] = xb_ref[0]
    dtpad[0:ts, :] = dt_ref[0]
    tail = xpad[8 + ts - (CONV_W - 1):8 + ts, :]
    convn_ref[0] = tail

    def conv_silu(c0, width):
        acc = cb_ref[:, c0:c0 + width]
        for k in range(CONV_W):
            r0 = 8 - (CONV_W - 1) + k
            acc = acc + cw_ref[k:k + 1, c0:c0 + width] * xpad[r0:r0 + q, c0:c0 + width]
        return _silu(acc)

    dt = _softplus(dtpad[...] + dtb_ref[...])
    if ts < q:
        rows = lax.broadcasted_iota(jnp.int32, (q, LANES), 0)
        dt = jnp.where(rows < ts, dt, 0.0)
    a = dt * (-jnp.exp(alog_ref[...]))
    row = lax.broadcasted_iota(jnp.int32, (q, q), 0)
    col = lax.broadcasted_iota(jnp.int32, (q, q), 1)
    causal = row >= col
    tril = jnp.where(causal, 1.0, 0.0).astype(BF16)
    a_hi, a_lo = _split_bf16(a)
    acs = _dot(tril, a_hi) + _dot(tril, a_lo)
    acs_t = acs.T
    dt_t = dt.T
    acs_last = acs[q - 1:q, :]
    w1_hi, w1_lo = _split_bf16(dt * jnp.exp(acs_last - acs))
    w2_hi, w2_lo = _split_bf16(jnp.exp(acs))
    cd = jnp.exp(acs_last)
    lane_g = lax.broadcasted_iota(jnp.int32, (1, gw), 1)
    head_masks = [jnp.where((lane_g >= r * hd) & (lane_g < (r + 1) * hd), 1.0, 0.0).astype(BF16)
                  for r in range(hpg)]

    for g in range(ng):
        xg = conv_silu(g * gw, gw)
        bb = conv_silu(di + g * ds, ds).astype(BF16)
        cc = conv_silu(di + gn + g * ds, ds).astype(BF16)
        cbm = _dot_nt(cc, bb)
        xgb = xg.astype(BF16)
        yd = jnp.zeros((q, gw), F32)
        for r in range(hpg):
            h = g * hpg + r
            seg = acs[:, h:h + 1] - acs_t[h:h + 1, :]
            dec = jnp.exp(jnp.where(causal, seg, NEG))
            mh = (cbm * dec * dt_t[h:h + 1, :]).astype(BF16)
            yd = yd + _dot(mh, xgb * head_masks[r])
        e_g = e_ref[:, g * gw:(g + 1) * gw]
        w1g = _dot(w1_hi, e_g) + _dot(w1_lo, e_g)
        w2g = _dot(w2_hi, e_g) + _dot(w2_lo, e_g)
        hg = hst[g * gw:(g + 1) * gw, :]
        yoff = _dot_nt(cc, hg.astype(BF16))
        yg = yd + yoff * w2g + dsk_ref[:, g * gw:(g + 1) * gw] * xg
        y_ref[0, :, g * gw:(g + 1) * gw] = yg[0:ts]
        st = _dot((xg * w1g).T.astype(BF16), bb)
        cdt = jnp.concatenate(
            [jnp.broadcast_to(cd[:, g * hpg + r:g * hpg + r + 1], (hd, ds)) for r in range(hpg)], axis=0)
        hst[g * gw:(g + 1) * gw, :] = hg * cdt + st

    xpad[8 - (CONV_W - 1):8, :] = tail

    @pl.when(s == pl.num_programs(1) - 1)
    def _():
        hn_ref[0] = hst[...]


def _ssd(zxbc3, dt3, conv0, h0, cw, cb, dtb, alog, dsk, emat, *, nh, hd, ng, ds):
    nb, length, _ = zxbc3.shape
    di = nh * hd
    cdim = di + 2 * ng * ds
    ts = min(CHUNK, length)
    ns = length // ts
    body = functools.partial(_ssd_body, ts=ts, nh=nh, hd=hd, ng=ng, ds=ds)
    assert cdim == 2 * di
    return pl.pallas_call(
        body,
        grid=(nb, ns),
        in_specs=[
            pl.BlockSpec((1, ts, di), lambda b, s: (b, s, 1)),
            pl.BlockSpec((1, ts, di), lambda b, s: (b, s, 2)),
            pl.BlockSpec((1, ts, LANES), lambda b, s: (b, s, 0)),
            pl.BlockSpec((1, CONV_W - 1, cdim), lambda b, s: (b, 0, 0)),
            pl.BlockSpec((1, di, ds), lambda b, s: (b, 0, 0)),
            pl.BlockSpec((CONV_W, cdim), lambda b, s: (0, 0)),
            pl.BlockSpec((1, cdim), lambda b, s: (0, 0)),
            pl.BlockSpec((1, LANES), lambda b, s: (0, 0)),
            pl.BlockSpec((1, LANES), lambda b, s: (0, 0)),
            pl.BlockSpec((1, di), lambda b, s: (0, 0)),
            pl.BlockSpec((LANES, di), lambda b, s: (0, 0)),
        ],
        out_specs=[
            pl.BlockSpec((1, ts, di), lambda b, s: (b, s, 0)),
            pl.BlockSpec((1, di, ds), lambda b, s: (b, 0, 0)),
            pl.BlockSpec((1, CONV_W - 1, cdim), lambda b, s: (b, 0, 0)),
        ],
        out_shape=[SDS((nb, length, di), F32), SDS((nb, di, ds), F32), SDS((nb, CONV_W - 1, cdim), F32)],
        scratch_shapes=[
            pltpu.VMEM((8 + CHUNK, cdim), F32),
            pltpu.VMEM((CHUNK, LANES), F32),
            pltpu.VMEM((di, ds), F32),
        ],
        compiler_params=_cparams(("parallel", "arbitrary")),
        name="ssd",
    )(zxbc3, zxbc3, dt3, conv0, h0, cw, cb, dtb, alog, dsk, emat)


def _post_body(y_ref, z_ref, x_ref, gw_ref, w_ref, o_ref, *, ng):
    yz = y_ref[...] * _silu(z_ref[...])
    width = yz.shape[1] // ng
    parts = []
    for g in range(ng):
        blk = yz[:, g * width:(g + 1) * width]
        ms = jnp.mean(blk * blk, axis=-1, keepdims=True)
        parts.append(blk * lax.rsqrt(ms + EPS))
    yn = jnp.concatenate(parts, axis=1) * gw_ref[...]
    o_ref[...] = x_ref[...] + _dot(yn.astype(BF16), w_ref[...])


def _post(y2, zxbc2, x2, gw, w_out, tm, ng):
    m, d = x2.shape
    di = y2.shape[1]
    return pl.pallas_call(
        functools.partial(_post_body, ng=ng),
        grid=(m // tm,),
        in_specs=[
            pl.BlockSpec((tm, di), lambda i: (i, 0)),
            pl.BlockSpec((tm, di), lambda i: (i, 0)),
            pl.BlockSpec((tm, d), lambda i: (i, 0)),
            pl.BlockSpec((1, di), lambda i: (0, 0)),
            pl.BlockSpec((di, d), lambda i: (0, 0)),
        ],
        out_specs=pl.BlockSpec((tm, d), lambda i: (i, 0)),
        out_shape=SDS((m, d), F32),
        compiler_params=_cparams(("parallel",)),
        name="post_mamba",
    )(y2, zxbc2, x2, gw, w_out)


def _mlp_body(x_ref, g_ref, wu_ref, wd_ref, o_ref, u_scr):
    c = pl.program_id(1)

    @pl.when(c == 0)
    def _():
        u_scr[...] = _rms(x_ref[...], g_ref[...]).astype(BF16)

    h = jnp.maximum(_dot(u_scr[...], wu_ref[...]), 0.0)
    part = _dot((h * h).astype(BF16), wd_ref[...])

    @pl.when(c == 0)
    def _():
        o_ref[...] = x_ref[...] + part

    @pl.when(c != 0)
    def _():
        o_ref[...] += part


def _mlp(x2, g, w_up, w_down, tm, tf):
    m, d = x2.shape
    ff = w_up.shape[1]
    return pl.pallas_call(
        _mlp_body,
        grid=(m // tm, ff // tf),
        in_specs=[
            pl.BlockSpec((tm, d), lambda i, c: (i, 0)),
            pl.BlockSpec((1, d), lambda i, c: (0, 0)),
            pl.BlockSpec((d, tf), lambda i, c: (0, c)),
            pl.BlockSpec((tf, d), lambda i, c: (c, 0)),
        ],
        out_specs=pl.BlockSpec((tm, d), lambda i, c: (i, 0)),
        out_shape=SDS((m, d), F32),
        scratch_shapes=[pltpu.VMEM((tm, d), BF16)],
        compiler_params=_cparams(("parallel", "arbitrary")),
        name="mlp",
    )(x2, g, w_up, w_down)


def _kv_body(x_ref, g_ref, wkt_ref, wv_ref, kn_ref, kt_ref, v_ref, *head_refs, nha, ahd):
    u = _rms(x_ref[0], g_ref[...]).astype(BF16)
    kt = _dot_nt(wkt_ref[...], u)
    kd, tm = kt.shape
    k3 = kt.reshape(kd // ahd, ahd, tm)
    ms = jnp.mean(k3 * k3, axis=1, keepdims=True)
    kt = (k3 * lax.rsqrt(ms + EPS)).reshape(kd, tm) * kn_ref[...]
    v = _dot(u, wv_ref[...])
    kt_ref[0] = kt
    v_ref[0] = v
    if head_refs:
        kht_ref, vh_ref = head_refs
        for h in range(nha):
            kht_ref[0, h] = kt[h * LANES:(h + 1) * LANES, :].astype(BF16)
            vh_ref[0, h] = v[:, h * LANES:(h + 1) * LANES].astype(BF16)


def _kv(x3, g, wk_t, wv, kn_col, tm, nha, ahd, head_major):
    nb, t, d = x3.shape
    kd = nha * LANES
    out_specs = [
        pl.BlockSpec((1, kd, tm), lambda b, i: (b, 0, i)),
        pl.BlockSpec((1, tm, kd), lambda b, i: (b, i, 0)),
    ]
    out_shape = [SDS((nb, kd, t), F32), SDS((nb, t, kd), F32)]
    if head_major:
        out_specs += [
            pl.BlockSpec((1, nha, LANES, tm), lambda b, i: (b, 0, 0, i)),
            pl.BlockSpec((1, nha, tm, LANES), lambda b, i: (b, 0, i, 0)),
        ]
        out_shape += [SDS((nb, nha, LANES, t), BF16), SDS((nb, nha, t, LANES), BF16)]
    return pl.pallas_call(
        functools.partial(_kv_body, nha=nha, ahd=ahd),
        grid=(nb, t // tm),
        in_specs=[
            pl.BlockSpec((1, tm, d), lambda b, i: (b, i, 0)),
            pl.BlockSpec((1, d), lambda b, i: (0, 0)),
            pl.BlockSpec((kd, d), lambda b, i: (0, 0)),
            pl.BlockSpec((d, kd), lambda b, i: (0, 0)),
            pl.BlockSpec((kd, 1), lambda b, i: (0, 0)),
        ],
        out_specs=out_specs,
        out_shape=out_shape,
        compiler_params=_cparams(("parallel", "parallel")),
        name="kv_proj",
    )(x3, g, wk_t, wv, kn_col)


def _q_body(x_ref, g_ref, w_ref, qn_ref, gm_ref, q_ref, *, nha, head_major, scale):
    u = _rms(x_ref[0], g_ref[...]).astype(BF16)
    qn = _group_norm64(_dot(u, w_ref[...]), gm_ref[...]) * qn_ref[...] * scale
    if head_major:
        for h in range(nha):
            q_ref[0, h] = qn[:, h * LANES:(h + 1) * LANES].astype(BF16)
    else:
        q_ref[0] = qn.astype(BF16)


def _qproj(x3, g, w_q, qn_t, gmat, tm, nha, head_major, scale):
    nb, t, d = x3.shape
    kd = nha * LANES
    if head_major:
        out_spec = pl.BlockSpec((1, nha, tm, LANES), lambda b, i: (b, 0, i, 0))
        out_shape = SDS((nb, nha, t, LANES), BF16)
    else:
        out_spec = pl.BlockSpec((1, tm, kd), lambda b, i: (b, i, 0))
        out_shape = SDS((nb, t, kd), BF16)
    return pl.pallas_call(
        functools.partial(_q_body, nha=nha, head_major=head_major, scale=scale),
        grid=(nb, t // tm),
        in_specs=[
            pl.BlockSpec((1, tm, d), lambda b, i: (b, i, 0)),
            pl.BlockSpec((1, d), lambda b, i: (0, 0)),
            pl.BlockSpec((d, kd), lambda b, i: (0, 0)),
            pl.BlockSpec((1, kd), lambda b, i: (0, 0)),
            pl.BlockSpec((256, 256), lambda b, i: (0, 0)),
        ],
        out_specs=out_spec,
        out_shape=out_shape,
        compiler_params=_cparams(("parallel", "parallel")),
        name="q_proj",
    )(x3, g, w_q, qn_t, gmat)


def _lam(l1, k1, l2, k2, lam_init):
    return (jnp.exp(jnp.sum(l1 * k1, axis=-1, keepdims=True))
            - jnp.exp(jnp.sum(l2 * k2, axis=-1, keepdims=True)) + lam_init)


def _attn_body(it_ref, jt_ref, q_ref, k_ref, v_ref, l1_ref, k1_ref, l2_ref, k2_ref, o_ref,
               m_scr, l_scr, acc_scr, *, tq, tk, nha, hd, lam_init):
    t = pl.program_id(1)
    i = it_ref[t]
    j = jt_ref[t]
    nkb = tk // LANES

    @pl.when(j == 0)
    def _():
        m_scr[...] = jnp.full_like(m_scr, NEG)
        l_scr[...] = jnp.zeros_like(l_scr)
        acc_scr[...] = jnp.zeros_like(acc_scr)

    lane = lax.broadcasted_iota(jnp.int32, (1, LANES), 1)
    map_masks = [jnp.where(lane < hd, 1.0, 0.0).astype(BF16), jnp.where(lane >= hd, 1.0, 0.0).astype(BF16)]

    def head(h, carry, bias):
        qh = q_ref[0, h]
        kh = k_ref[0, h]
        vh = v_ref[0, h]
        for c in range(2):
            sc = _dot(qh * map_masks[c], kh)
            if bias is not None:
                sc = sc + bias
            m_prev = m_scr[h, c]
            blk_max = sc[:, 0:LANES]
            for b in range(1, nkb):
                blk_max = jnp.maximum(blk_max, sc[:, b * LANES:(b + 1) * LANES])
            m_new = jnp.maximum(m_prev, jnp.max(blk_max, axis=-1, keepdims=True))
            alpha = jnp.exp(m_prev - m_new)
            p = jnp.exp(sc - pltpu.repeat(m_new, nkb, axis=1))
            psum = p[:, 0:LANES]
            for b in range(1, nkb):
                psum = psum + p[:, b * LANES:(b + 1) * LANES]
            l_scr[h, c] = alpha * l_scr[h, c] + psum
            acc_scr[h, c] = alpha * acc_scr[h, c] + _dot(p.astype(BF16), vh)
            m_scr[h, c] = m_new
        return carry

    needs_mask = (j + 1) * tk - 1 > i * tq

    @pl.when(needs_mask)
    def _():
        qpos = i * tq + lax.broadcasted_iota(jnp.int32, (tq, tk), 0)
        kpos = j * tk + lax.broadcasted_iota(jnp.int32, (tq, tk), 1)
        bias = jnp.where(kpos <= qpos, 0.0, NEG)
        lax.fori_loop(0, nha, functools.partial(head, bias=bias), 0)

    @pl.when(jnp.logical_not(needs_mask))
    def _():
        lax.fori_loop(0, nha, functools.partial(head, bias=None), 0)

    @pl.when((j + 1) * tk >= (i + 1) * tq)
    def _():
        lam = _lam(l1_ref[...], k1_ref[...], l2_ref[...], k2_ref[...], lam_init)
        for h in range(nha):
            l0 = jnp.sum(l_scr[h, 0], axis=-1, keepdims=True)
            l1 = jnp.sum(l_scr[h, 1], axis=-1, keepdims=True)
            o_ref[0, :, h * LANES:(h + 1) * LANES] = (
                acc_scr[h, 0] * (1.0 / l0) - (lam * (1.0 / l1)) * acc_scr[h, 1])


def _attn_prompt(qh, kh, vh, l1, k1, l2, k2, tq, tk, hd, lam_init):
    nb, nha, t, _ = qh.shape
    its, jts = [], []
    for i in range(t // tq):
        for j in range(((i + 1) * tq + tk - 1) // tk):
            its.append(i)
            jts.append(j)
    it = jnp.asarray(its, jnp.int32)
    jt = jnp.asarray(jts, jnp.int32)
    lspec = pl.BlockSpec((1, hd), lambda b, s, it, jt: (0, 0))
    grid_spec = pltpu.PrefetchScalarGridSpec(
        num_scalar_prefetch=2,
        grid=(nb, len(its)),
        in_specs=[
            pl.BlockSpec((1, nha, tq, LANES), lambda b, s, it, jt: (b, 0, it[s], 0)),
            pl.BlockSpec((1, nha, LANES, tk), lambda b, s, it, jt: (b, 0, 0, jt[s])),
            pl.BlockSpec((1, nha, tk, LANES), lambda b, s, it, jt: (b, 0, jt[s], 0)),
            lspec, lspec, lspec, lspec,
        ],
        out_specs=pl.BlockSpec((1, tq, nha * LANES), lambda b, s, it, jt: (b, it[s], 0)),
        scratch_shapes=[
            pltpu.VMEM((nha, 2, tq, LANES), F32),
            pltpu.VMEM((nha, 2, tq, LANES), F32),
            pltpu.VMEM((nha, 2, tq, LANES), F32),
        ],
    )
    return pl.pallas_call(
        functools.partial(_attn_body, tq=tq, tk=tk, nha=nha, hd=hd, lam_init=lam_init),
        grid_spec=grid_spec,
        out_shape=SDS((nb, t, nha * LANES), F32),
        compiler_params=_cparams(("parallel", "arbitrary")),
        name="attn_prompt",
    )(it, jt, qh, kh, vh, l1, k1, l2, k2)


def _decode_body(pt_ref, q_ref, kn_ref, vn_ref, l1_ref, k1_ref, l2_ref, k2_ref, *rest,
                 pps, tn, nha, hd, lam_init):
    k_refs = rest[:pps]
    v_refs = rest[pps:2 * pps]
    o_ref = rest[2 * pps]
    qb, m_scr, l_scr, acc_scr = rest[2 * pps + 1:]
    b_idx = pl.program_id(0)
    j = pl.program_id(1)
    nmap = 2 * nha
    nrow = tn * nmap
    kd = nha * LANES
    page = k_refs[0].shape[2]

    def online(sc, vals):
        n = len(vals)
        m_prev = m_scr[...]
        blk_max = sc[:, 0:LANES]
        for b in range(1, n):
            blk_max = jnp.maximum(blk_max, sc[:, b * LANES:(b + 1) * LANES])
        m_new = jnp.maximum(m_prev, jnp.max(blk_max, axis=-1, keepdims=True))
        alpha = jnp.exp(m_prev - m_new)
        p = jnp.exp(sc - pltpu.repeat(m_new, n, axis=1))
        psum = p[:, 0:LANES]
        for b in range(1, n):
            psum = psum + p[:, b * LANES:(b + 1) * LANES]
        l_scr[...] = alpha * l_scr[...] + psum
        pv = _dot(p[:, 0:LANES].astype(BF16), vals[0])
        for b in range(1, n):
            pv = pv + _dot(p[:, b * LANES:(b + 1) * LANES].astype(BF16), vals[b])
        acc_scr[...] = pltpu.repeat(alpha, kd // LANES, axis=1) * acc_scr[...] + pv
        m_scr[...] = m_new

    def page_values(v_ref):
        return jnp.concatenate(
            [v_ref[0, pl.ds(h, page, stride=nha), :] for h in range(nha)], axis=1).astype(BF16)

    @pl.when(j == 0)
    def _():
        rid = lax.broadcasted_iota(jnp.int32, (nmap, kd), 0)
        lid = lax.broadcasted_iota(jnp.int32, (nmap, kd), 1)
        sel = (lid >= rid * hd) & (lid < (rid + 1) * hd)
        qf = q_ref[0].astype(F32)
        for tt in range(tn):
            rowq = jnp.broadcast_to(qf[tt:tt + 1, :], (nmap, kd))
            qb[tt * nmap:(tt + 1) * nmap, :] = jnp.where(sel, rowq, 0.0).astype(BF16)
        m_scr[...] = jnp.full_like(m_scr, NEG)
        l_scr[...] = jnp.zeros_like(l_scr)
        acc_scr[...] = jnp.zeros_like(acc_scr)
        nnew = kn_ref.shape[2]
        qrow = lax.broadcasted_iota(jnp.int32, (nrow, LANES), 0)
        ktok = lax.broadcasted_iota(jnp.int32, (nrow, LANES), 1) - b_idx * tn
        for blk in range(nnew // LANES):
            sc = _dot(qb[...], kn_ref[0, :, blk * LANES:(blk + 1) * LANES].astype(BF16))
            kt = ktok - blk * LANES
            sc = jnp.where((kt >= 0) & (kt * nmap <= qrow), sc, NEG)
            online(sc, [vn_ref[0, blk * LANES:(blk + 1) * LANES, :].astype(BF16)])

    scs = [_dot(qb[...], k_refs[b][0].astype(BF16)) for b in range(pps)]
    online(jnp.concatenate(scs, axis=1), [page_values(v_refs[b]) for b in range(pps)])

    @pl.when(j == pl.num_programs(1) - 1)
    def _():
        lam = _lam(l1_ref[...], k1_ref[...], l2_ref[...], k2_ref[...], lam_init)
        inv_l = 1.0 / jnp.sum(l_scr[...], axis=-1, keepdims=True)
        rid = lax.broadcasted_iota(jnp.int32, (nmap, 1), 0)
        rid2 = lax.broadcasted_iota(jnp.int32, (nmap, kd), 0)
        lid2 = lax.broadcasted_iota(jnp.int32, (nmap, kd), 1)
        own = (lid2 >= (rid2 >> 1) * LANES) & (lid2 < ((rid2 >> 1) + 1) * LANES)
        for tt in range(tn):
            il = inv_l[tt * nmap:(tt + 1) * nmap, :]
            coef = jnp.where((rid & 1) == 0, il, -lam * il)
            w = jnp.where(own, acc_scr[tt * nmap:(tt + 1) * nmap, :] * coef, 0.0)
            o_ref[0, tt:tt + 1, :] = jnp.sum(w, axis=0, keepdims=True)


def _attn_sample(page_table, q3, kt_new, v_new, cache_kt, cache_v2, l1, k1, l2, k2, pps, hd, lam_init):
    nb, tn, kd = q3.shape
    nha = kd // LANES
    n_pages = page_table.shape[1]
    page = cache_kt.shape[2]
    nnew = kt_new.shape[2]
    nrow = tn * 2 * nha
    assert page == LANES and nnew == LANES and nb * tn == nnew
    pt_flat = page_table.reshape(-1)

    def k_spec(b_off):
        return pl.BlockSpec((1, kd, page),
                            lambda b, j, pt: (pt[b * n_pages + j * pps + b_off], 0, 0))

    def v_spec(b_off):
        return pl.BlockSpec((1, page * nha, LANES),
                            lambda b, j, pt: (pt[b * n_pages + j * pps + b_off], 0, 0))

    lspec = pl.BlockSpec((1, hd), lambda b, j, pt: (0, 0))
    grid_spec = pltpu.PrefetchScalarGridSpec(
        num_scalar_prefetch=1,
        grid=(nb, n_pages // pps),
        in_specs=[
            pl.BlockSpec((1, tn, kd), lambda b, j, pt: (b, 0, 0)),
            pl.BlockSpec((1, kd, nnew), lambda b, j, pt: (0, 0, 0)),
            pl.BlockSpec((1, nnew, kd), lambda b, j, pt: (0, 0, 0)),
            lspec, lspec, lspec, lspec]
        + [k_spec(b) for b in range(pps)] + [v_spec(b) for b in range(pps)],
        out_specs=pl.BlockSpec((1, tn, kd), lambda b, j, pt: (b, 0, 0)),
        scratch_shapes=[
            pltpu.VMEM((nrow, kd), BF16),
            pltpu.VMEM((nrow, LANES), F32),
            pltpu.VMEM((nrow, LANES), F32),
            pltpu.VMEM((nrow, kd), F32),
        ],
    )
    return pl.pallas_call(
        functools.partial(_decode_body, pps=pps, tn=tn, nha=nha, hd=hd, lam_init=lam_init),
        grid_spec=grid_spec,
        out_shape=SDS((nb, tn, kd), F32),
        compiler_params=_cparams(("parallel", "arbitrary")),
        name="attn_sample",
    )(pt_flat, q3, kt_new, v_new, l1, k1, l2, k2, *([cache_kt] * pps), *([cache_v2] * pps))


def _attn_out_body(o_ref, x_ref, sw_ref, w_ref, y_ref, *, nha, post_scale):
    o = o_ref[...]
    parts = []
    for h in range(nha):
        blk = o[:, h * LANES:(h + 1) * LANES]
        ms = jnp.mean(blk * blk, axis=-1, keepdims=True)
        parts.append(blk * lax.rsqrt(ms + EPS))
    on = jnp.concatenate(parts, axis=1) * sw_ref[...] * post_scale
    y_ref[...] = x_ref[...] + _dot(on.astype(BF16), w_ref[...])


def _attn_out(o2, x2, sw_t, w_o, tm, nha, post_scale):
    m, d = x2.shape
    vd = o2.shape[1]
    return pl.pallas_call(
        functools.partial(_attn_out_body, nha=nha, post_scale=post_scale),
        grid=(m // tm,),
        in_specs=[
            pl.BlockSpec((tm, vd), lambda i: (i, 0)),
            pl.BlockSpec((tm, d), lambda i: (i, 0)),
            pl.BlockSpec((1, vd), lambda i: (0, 0)),
            pl.BlockSpec((vd, d), lambda i: (0, 0)),
        ],
        out_specs=pl.BlockSpec((tm, d), lambda i: (i, 0)),
        out_shape=SDS((m, d), F32),
        compiler_params=_cparams(("parallel",)),
        name="attn_out",
    )(o2, x2, sw_t, w_o)


def kernel(x_prompt, x_sample, state_ssm, state_conv, cache_k, cache_v, page_table, norm_a, w_in_a, conv_w, conv_b, dt_bias, a_log, d_skip, gnorm_w, w_out_a, kv_norm, w_kv, k_norm, norm_b, w_q, q_norm, lam_q1, lam_k1, lam_q2, lam_k2, subln, w_o, norm_m, w_up, w_down):
    n_a = norm_a.shape[0]
    n_b = norm_b.shape[0]
    d = x_prompt.shape[-1]
    nh, hd, ds = state_ssm.shape[2], state_ssm.shape[3], state_ssm.shape[4]
    di = nh * hd
    cdim = state_conv.shape[-1]
    ng = (cdim - di) // (2 * ds)
    nha, ahd = cache_k.shape[2], cache_k.shape[4]
    kd = nha * 2 * ahd
    page = cache_k.shape[1]
    assert 2 * ahd == LANES and cache_v.shape[3] == LANES and nh <= LANES

    emat = (jnp.arange(LANES)[:, None] == (jnp.arange(di)[None, :] // hd)).astype(BF16)
    gidx = jnp.arange(256) // ahd
    gmat = jnp.where(gidx[:, None] == gidx[None, :], 1.0 / ahd, 0.0).astype(BF16)

    def row(v):
        return v.reshape(1, -1).astype(F32)

    def pad_lanes(v):
        return jnp.pad(v.astype(F32), (0, LANES - v.shape[0])).reshape(1, LANES)

    w_main_b = [w_in_a[l][:, :di + cdim].astype(BF16) for l in range(n_a)]
    w_dt_b = [jnp.pad(w_in_a[l][:, di + cdim:], ((0, 0), (0, LANES - nh))).astype(BF16) for l in range(n_a)]
    w_out_b = w_out_a.astype(BF16)
    wk_t_b = w_kv[:, :kd].T.astype(BF16)
    wv_b = w_kv[:, kd:].astype(BF16)
    w_q_b = w_q.astype(BF16)
    w_o_b = w_o.astype(BF16)
    w_up_b = w_up.astype(BF16)
    w_down_b = w_down.astype(BF16)
    scale = ahd ** -0.5

    def trunk(x3, conv0, ssm0, attend, seq_major):
        nb, t, _ = x3.shape
        m = nb * t
        tm = min(512, m)
        x2 = x3.reshape(m, d)
        ab, at = (nb, t) if seq_major else (1, m)
        tma = min(512, at)
        convs, ssms = [], []
        k_new = v_new = kv_ctx = None
        for l in range(n_a + n_b):
            if l < n_a:
                zxbc, dtr = _inproj(x2, row(norm_a[l]), w_main_b[l], w_dt_b[l], tm, di)
                y3, hn, cn = _ssd(
                    zxbc.reshape(nb, t, di + cdim), dtr.reshape(nb, t, LANES),
                    conv0[l], ssm0[l].reshape(nb, di, ds),
                    conv_w[l], row(conv_b[l]), pad_lanes(dt_bias[l]), pad_lanes(a_log[l]),
                    row(jnp.repeat(d_skip[l], hd)), emat, nh=nh, hd=hd, ng=ng, ds=ds)
                convs.append(cn)
                ssms.append(hn.reshape(nb, nh, hd, ds))
                x2 = _post(y3.reshape(m, di), zxbc, x2, row(gnorm_w[l]), w_out_b[l], min(256, m), ng)
            else:
                jb = l - n_a
                if l == n_a:
                    kv_ctx = _kv(x2.reshape(ab, at, d), row(kv_norm), wk_t_b, wv_b,
                                 jnp.tile(k_norm, 2 * nha).reshape(kd, 1).astype(F32), tma, nha, ahd, seq_major)
                    k_new = jnp.transpose(kv_ctx[0].reshape(ab, nha, 2, ahd, at), (0, 4, 1, 2, 3))
                    k_new = k_new.reshape(nb, t, nha, 2, ahd)
                    v_new = kv_ctx[1].reshape(nb, t, nha, LANES)
                lam_init = 0.8 - 0.6 * math.exp(-0.3 * l)
                lams = (row(lam_q1[jb]), row(lam_k1[jb]), row(lam_q2[jb]), row(lam_k2[jb]))
                qout = _qproj(x2.reshape(ab, at, d), row(norm_b[jb]), w_q_b[jb],
                              row(jnp.tile(q_norm[jb], 2 * nha)), gmat, tma, nha, seq_major, scale)
                o3 = attend(qout, kv_ctx, lams, lam_init, nb, t)
                x2 = _attn_out(o3.reshape(m, kd), x2, row(jnp.tile(subln[jb], nha)), w_o_b[jb],
                               tm, nha, 1.0 - lam_init)
            x2 = _mlp(x2, row(norm_m[l]), w_up_b[l], w_down_b[l], min(1024, m), 1024)
        return x2.reshape(nb, t, d), jnp.stack(ssms), jnp.stack(convs), k_new, v_new

    def attend_prompt(qh, kv_ctx, lams, lam_init, nb, t):
        return _attn_prompt(qh, kv_ctx[2], kv_ctx[3], *lams, 512, 512, ahd, lam_init)

    n_pool = cache_k.shape[0]
    cache_kt = jnp.transpose(cache_k, (0, 2, 3, 4, 1)).reshape(n_pool, kd, page)
    cache_v2 = cache_v.reshape(n_pool, page * nha, LANES)

    def attend_sample(q3, kv_ctx, lams, lam_init, nb, t):
        return _attn_sample(page_table, q3.reshape(nb, t, kd), kv_ctx[0], kv_ctx[1],
                            cache_kt, cache_v2, *lams, 8, ahd, lam_init)

    bp = x_prompt.shape[0]
    conv0_p = jnp.zeros((n_a, bp, CONV_W - 1, cdim), F32)
    ssm0_p = jnp.zeros((n_a, bp, nh, hd, ds), F32)
    y_p, ssm_p, conv_p, k_p, v_p = trunk(x_prompt, conv0_p, ssm0_p, attend_prompt, True)
    y_s, ssm_s, conv_s, k_s, v_s = trunk(x_sample, state_conv, state_ssm, attend_sample, False)
    return (y_p, y_s, ssm_p, conv_p, k_p, v_p, ssm_s, conv_s, k_s, v_s)
```

```python
import functools
import math

import jax
import jax.numpy as jnp
from jax import lax
from jax.experimental import pallas as pl
from jax.experimental.pallas import tpu as pltpu

F32 = jnp.float32
BF16 = jnp.bfloat16
EPS = 1e-6
LANES = 128
CHUNK = 128
CONV_W = 4
VMEM_LIMIT = 48 * 1024 * 1024
NEG = -1e30
LOG2E = math.log2(math.e)
SDS = jax.ShapeDtypeStruct


def _cparams(sem):
    return pltpu.CompilerParams(dimension_semantics=sem, vmem_limit_bytes=VMEM_LIMIT)


def _rms(x, g):
    ms = jnp.mean(x * x, axis=-1, keepdims=True)
    return x * lax.rsqrt(ms + EPS) * g


def _silu(x):
    return x * (1.0 / (1.0 + jnp.exp(-x)))


def _softplus(x):
    e = jnp.exp(-jnp.abs(x))
    u = 1.0 + e
    log1p_e = jnp.where(u == 1.0, e, jnp.log(u) * (e / (u - 1.0)))
    return jnp.maximum(x, 0.0) + log1p_e


def _dot(a, b):
    return jnp.dot(a, b, preferred_element_type=F32)


def _dot_nt(a, b):
    return lax.dot_general(a, b, (((1,), (1,)), ((), ())), preferred_element_type=F32)


def _split_bf16(x):
    hi = x.astype(BF16)
    lo = (x - hi.astype(F32)).astype(BF16)
    return hi, lo


def _group_norm64(x, gmat):
    parts = []
    for c in range(x.shape[1] // 256):
        blk = x[:, c * 256:(c + 1) * 256]
        ms = _dot((blk * blk).astype(BF16), gmat)
        parts.append(blk * lax.rsqrt(ms + EPS))
    return jnp.concatenate(parts, axis=1)


def _inproj_body(x_ref, g_ref, w_ref, o_ref):
    u = _rms(x_ref[...], g_ref[...]).astype(BF16)
    o_ref[...] = _dot(u, w_ref[...])


def _inproj(x2, g, w_all, tm, tn):
    m, d = x2.shape
    n = w_all.shape[1]
    return pl.pallas_call(
        _inproj_body,
        grid=(n // tn, m // tm),
        in_specs=[
            pl.BlockSpec((tm, d), lambda j, i: (i, 0)),
            pl.BlockSpec((1, d), lambda j, i: (0, 0)),
            pl.BlockSpec((d, tn), lambda j, i: (0, j)),
        ],
        out_specs=pl.BlockSpec((tm, tn), lambda j, i: (i, j)),
        out_shape=SDS((m, n), F32),
        compiler_params=_cparams(("parallel", "parallel")),
        name="inproj",
    )(x2, g, w_all)


def _ssd_body(xa_ref, xb_ref, dt_ref, conv0_ref, h0_ref, cw_ref, cb_ref, dtb_ref, alog_ref,
              dsk_ref, e_ref, y_ref, hn_ref, convn_ref, xpad, dtpad, hst, wsp,
              *, ts, nh, hd, ng, ds):
    q = CHUNK
    nph = q // 8
    di = nh * hd
    gn = ng * ds
    hpg = nh // ng
    gw = hpg * hd
    nxs = di // LANES
    ncs = xpad.shape[0]
    pad = CONV_W - 1
    s = pl.program_id(1)

    @pl.when(s == 0)
    def _():
        xpad[...] = jnp.zeros_like(xpad)
        dtpad[...] = jnp.zeros_like(dtpad)
        for c in range(ncs):
            xpad[c, 8 - pad:8, :] = conv0_ref[0, :, c * LANES:(c + 1) * LANES]
        hst[...] = h0_ref[0]

    for c in range(ncs):
        src = xa_ref if c < nxs else xb_ref
        cc0 = (c % nxs) * LANES
        xpad[c, 8:8 + ts, :] = src[0, :, cc0:cc0 + LANES]
        convn_ref[0, :, c * LANES:(c + 1) * LANES] = xpad[c, 8 + ts - pad:8 + ts, :]
    dtpad[0:ts, :] = dt_ref[0]

    def phase_rows(ref2d, r0):
        return jnp.concatenate([ref2d[pl.ds(r0 + ph, nph, stride=8), :] for ph in range(8)], axis=0)

    def conv_silu(c0, width):
        slabs = []
        for c in range(c0 // LANES, (c0 + width) // LANES):
            acc = cb_ref[:, c * LANES:(c + 1) * LANES]
            for k in range(CONV_W):
                acc = acc + cw_ref[k:k + 1, c * LANES:(c + 1) * LANES] * phase_rows(xpad.at[c], 8 - pad + k)
            slabs.append(_silu(acc))
        return slabs[0] if len(slabs) == 1 else jnp.concatenate(slabs, axis=1)

    def real_time(pos):
        return ((pos & (nph - 1)) << 3) | (pos >> (nph.bit_length() - 1))

    def front(g):
        xg = conv_silu(g * gw, gw)
        bb = conv_silu(di + g * ds, ds).astype(BF16)
        cc = conv_silu(di + gn + g * ds, ds).astype(BF16)
        cbm = _dot_nt(cc, bb)
        hg = hst[g * gw:(g + 1) * gw, :]
        yoff = _dot_nt(cc, hg.astype(BF16))
        e_g = e_ref[:, g * gw:(g + 1) * gw]
        w1g = _dot(wsp[0], e_g)
        w2g = _dot(wsp[1], e_g)
        return xg, bb, cbm, hg, yoff, w1g, w2g

    dt = _softplus(phase_rows(dtpad, 0) + dtb_ref[...])
    if ts < q:
        dt = jnp.where(real_time(lax.broadcasted_iota(jnp.int32, (q, LANES), 0)) < ts, dt, 0.0)
    a = dt * (-jnp.exp(alog_ref[...]))
    causal = (real_time(lax.broadcasted_iota(jnp.int32, (q, q), 0))
              >= real_time(lax.broadcasted_iota(jnp.int32, (q, q), 1)))
    tril = jnp.where(causal, 1.0, 0.0).astype(BF16)

    a_hi, a_lo = _split_bf16(a)
    acs = _dot(jnp.concatenate([tril, tril], axis=1),
               jnp.concatenate([a_hi, a_lo], axis=0))
    acs2 = acs * LOG2E
    acs2_t = acs2.T
    dt_t = dt.T
    acs_last = acs[q - 1:q, :]
    wsp[0] = jnp.concatenate(_split_bf16(dt * jnp.exp(acs_last - acs)), axis=1)
    wsp[1] = jnp.concatenate(_split_bf16(jnp.exp(acs)), axis=1)
    cd = jnp.exp(acs_last)
    lane_g = lax.broadcasted_iota(jnp.int32, (1, gw), 1)

    nxt = front(0)
    for g in range(ng):
        xg, bb, cbm, hg, yoff, w1g, w2g = nxt
        if g + 1 < ng:
            nxt = front(g + 1)
        xgb = xg.astype(BF16)
        yd = jnp.zeros((q, gw), F32)
        for r in range(hpg):
            h = g * hpg + r
            seg = acs2[:, h:h + 1] - acs2_t[h:h + 1, :]
            dec = jnp.exp2(jnp.where(causal, seg, NEG))
            mh = (cbm * dec * dt_t[h:h + 1, :]).astype(BF16)
            yd = jnp.where((lane_g >= r * hd) & (lane_g < (r + 1) * hd), _dot(mh, xgb), yd)
        yg = yd + yoff * w2g + dsk_ref[:, g * gw:(g + 1) * gw] * xg
        for jj in range(gw // LANES):
            slab = yg[:, jj * LANES:(jj + 1) * LANES]
            cy = g * (gw // LANES) + jj
            if ts == q:
                for ph in range(8):
                    y_ref[0, cy, pl.ds(ph, nph, stride=8), :] = slab[ph * nph:(ph + 1) * nph]
            else:
                for tt in range(ts):
                    pos = (tt % 8) * nph + tt // 8
                    y_ref[0, cy, tt:tt + 1, :] = slab[pos:pos + 1]
        st = _dot((xg * w1g).T.astype(BF16), bb)
        cdt = jnp.concatenate(
            [jnp.broadcast_to(cd[:, g * hpg + r:g * hpg + r + 1], (hd, ds)) for r in range(hpg)], axis=0)
        hst[g * gw:(g + 1) * gw, :] = hg * cdt + st

    for c in range(ncs):
        xpad[c, 8 - pad:8, :] = xpad[c, 8 + ts - pad:8 + ts, :]

    @pl.when(s == pl.num_programs(1) - 1)
    def _():
        hn_ref[0] = hst[...]


def _ssd(zxbc3, conv0, h0, cw, cb, dtb, alog, dsk, emat, *, nh, hd, ng, ds):
    nb, length, _ = zxbc3.shape
    di = nh * hd
    cdim = di + 2 * ng * ds
    ts = min(CHUNK, length)
    ns = length // ts
    body = functools.partial(_ssd_body, ts=ts, nh=nh, hd=hd, ng=ng, ds=ds)
    assert cdim == 2 * di
    dt_blk = (di + cdim) // LANES
    return pl.pallas_call(
        body,
        grid=(nb, ns),
        in_specs=[
            pl.BlockSpec((1, ts, di), lambda b, s: (b, s, 1)),
            pl.BlockSpec((1, ts, di), lambda b, s: (b, s, 2)),
            pl.BlockSpec((1, ts, LANES), lambda b, s: (b, s, dt_blk)),
            pl.BlockSpec((1, CONV_W - 1, cdim), lambda b, s: (b, 0, 0)),
            pl.BlockSpec((1, di, ds), lambda b, s: (b, 0, 0)),
            pl.BlockSpec((CONV_W, cdim), lambda b, s: (0, 0)),
            pl.BlockSpec((1, cdim), lambda b, s: (0, 0)),
            pl.BlockSpec((1, LANES), lambda b, s: (0, 0)),
            pl.BlockSpec((1, LANES), lambda b, s: (0, 0)),
            pl.BlockSpec((1, di), lambda b, s: (0, 0)),
            pl.BlockSpec((2 * LANES, di), lambda b, s: (0, 0)),
        ],
        out_specs=[
            pl.BlockSpec((1, di // LANES, ts, LANES), lambda b, s: (b, 0, s, 0)),
            pl.BlockSpec((1, di, ds), lambda b, s: (b, 0, 0)),
            pl.BlockSpec((1, CONV_W - 1, cdim), lambda b, s: (b, 0, 0)),
        ],
        out_shape=[SDS((nb, di // LANES, length, LANES), F32), SDS((nb, di, ds), F32),
                   SDS((nb, CONV_W - 1, cdim), F32)],
        scratch_shapes=[
            pltpu.VMEM((cdim // LANES, 8 + CHUNK, LANES), F32),
            pltpu.VMEM((CHUNK, LANES), F32),
            pltpu.VMEM((di, ds), F32),
            pltpu.VMEM((2, CHUNK, 2 * LANES), BF16),
        ],
        compiler_params=_cparams(("parallel", "arbitrary")),
        name="ssd",
    )(zxbc3, zxbc3, zxbc3, conv0, h0, cw, cb, dtb, alog, dsk, emat)


def _post_body(y_ref, z_ref, x_ref, gw_ref, w_ref, o_ref, *, ng):
    if len(y_ref.shape) == 4:
        y = jnp.concatenate([y_ref[0, c] for c in range(y_ref.shape[1])], axis=1)
    else:
        y = y_ref[...]
    yz = y * _silu(z_ref[...])
    width = yz.shape[1] // ng
    parts = []
    for g in range(ng):
        blk = yz[:, g * width:(g + 1) * width]
        ms = jnp.mean(blk * blk, axis=-1, keepdims=True)
        parts.append(blk * lax.rsqrt(ms + EPS))
    yn = jnp.concatenate(parts, axis=1) * gw_ref[...]
    o_ref[...] = x_ref[...] + _dot(yn.astype(BF16), w_ref[...])


def _post(y, zxbc2, x2, gw, w_out, tm, ng):
    m, d = x2.shape
    if y.ndim == 4:
        nslab, length = y.shape[1], y.shape[2]
        di = nslab * LANES
        per_seq = length // tm
        y_spec = pl.BlockSpec((1, nslab, tm, LANES), lambda i: (i // per_seq, 0, i % per_seq, 0))
    else:
        di = y.shape[1]
        y_spec = pl.BlockSpec((tm, di), lambda i: (i, 0))
    return pl.pallas_call(
        functools.partial(_post_body, ng=ng),
        grid=(m // tm,),
        in_specs=[
            y_spec,
            pl.BlockSpec((tm, di), lambda i: (i, 0)),
            pl.BlockSpec((tm, d), lambda i: (i, 0)),
            pl.BlockSpec((1, di), lambda i: (0, 0)),
            pl.BlockSpec((di, d), lambda i: (0, 0)),
        ],
        out_specs=pl.BlockSpec((tm, d), lambda i: (i, 0)),
        out_shape=SDS((m, d), F32),
        compiler_params=_cparams(("parallel",)),
        name="post_mamba",
    )(y, zxbc2, x2, gw, w_out)


def _mlp_body(x_ref, g_ref, wu_ref, wd_ref, o_ref, u_scr):
    c = pl.program_id(1)

    @pl.when(c == 0)
    def _():
        u_scr[...] = _rms(x_ref[...], g_ref[...]).astype(BF16)

    h = jnp.maximum(_dot(u_scr[...], wu_ref[...]), 0.0)
    part = _dot((h * h).astype(BF16), wd_ref[...])

    @pl.when(c == 0)
    def _():
        o_ref[...] = x_ref[...] + part

    @pl.when(c != 0)
    def _():
        o_ref[...] += part


def _mlp(x2, g, w_up, w_down, tm, tf):
    m, d = x2.shape
    ff = w_up.shape[1]
    return pl.pallas_call(
        _mlp_body,
        grid=(m // tm, ff // tf),
        in_specs=[
            pl.BlockSpec((tm, d), lambda i, c: (i, 0)),
            pl.BlockSpec((1, d), lambda i, c: (0, 0)),
            pl.BlockSpec((d, tf), lambda i, c: (0, c)),
            pl.BlockSpec((tf, d), lambda i, c: (c, 0)),
        ],
        out_specs=pl.BlockSpec((tm, d), lambda i, c: (i, 0)),
        out_shape=SDS((m, d), F32),
        scratch_shapes=[pltpu.VMEM((tm, d), BF16)],
        compiler_params=_cparams(("parallel", "arbitrary")),
        name="mlp",
    )(x2, g, w_up, w_down)


def _row_group_norm(xt, width):
    n, tm = xt.shape
    x3 = xt.reshape(n // width, width, tm)
    ms = jnp.mean(x3 * x3, axis=1, keepdims=True)
    return (x3 * lax.rsqrt(ms + EPS)).reshape(n, tm)


def _kv_body(x_ref, g_ref, wkt_ref, wv_ref, kn_ref, kt_ref, v_ref, *head_refs, nha, ahd):
    u = _rms(x_ref[0], g_ref[...]).astype(BF16)
    kt = _row_group_norm(_dot_nt(wkt_ref[...], u), ahd) * kn_ref[...]
    v = _dot(u, wv_ref[...])
    kt_ref[0] = kt
    v_ref[0] = v
    if head_refs:
        kh_ref, vht_ref = head_refs
        k = kt.T
        for h in range(nha):
            kh_ref[0, h] = k[:, h * LANES:(h + 1) * LANES].astype(BF16)
            vht_ref[0, h] = v[:, h * LANES:(h + 1) * LANES].T.astype(BF16)


def _kv(x3, g, wk_t, wv, kn_col, tm, nha, ahd, head_major):
    nb, t, d = x3.shape
    kd = nha * LANES
    out_specs = [
        pl.BlockSpec((1, kd, tm), lambda b, i: (b, 0, i)),
        pl.BlockSpec((1, tm, kd), lambda b, i: (b, i, 0)),
    ]
    out_shape = [SDS((nb, kd, t), F32), SDS((nb, t, kd), F32)]
    if head_major:
        out_specs += [
            pl.BlockSpec((1, nha, tm, LANES), lambda b, i: (b, 0, i, 0)),
            pl.BlockSpec((1, nha, LANES, tm), lambda b, i: (b, 0, 0, i)),
        ]
        out_shape += [SDS((nb, nha, t, LANES), BF16), SDS((nb, nha, LANES, t), BF16)]
    return pl.pallas_call(
        functools.partial(_kv_body, nha=nha, ahd=ahd),
        grid=(nb, t // tm),
        in_specs=[
            pl.BlockSpec((1, tm, d), lambda b, i: (b, i, 0)),
            pl.BlockSpec((1, d), lambda b, i: (0, 0)),
            pl.BlockSpec((kd, d), lambda b, i: (0, 0)),
            pl.BlockSpec((d, kd), lambda b, i: (0, 0)),
            pl.BlockSpec((kd, 1), lambda b, i: (0, 0)),
        ],
        out_specs=out_specs,
        out_shape=out_shape,
        compiler_params=_cparams(("parallel", "parallel")),
        name="kv_proj",
    )(x3, g, wk_t, wv, kn_col)


def _q_body(x_ref, g_ref, w_ref, qn_ref, gm_ref, q_ref, *, scale):
    u = _rms(x_ref[0], g_ref[...]).astype(BF16)
    qn = _group_norm64(_dot(u, w_ref[...]), gm_ref[...]) * qn_ref[...] * scale
    q_ref[0] = qn.astype(BF16)


def _qproj(x3, g, w_q, qn_t, gmat, tm, nha, scale):
    nb, t, d = x3.shape
    kd = nha * LANES
    return pl.pallas_call(
        functools.partial(_q_body, scale=scale),
        grid=(nb, t // tm),
        in_specs=[
            pl.BlockSpec((1, tm, d), lambda b, i: (b, i, 0)),
            pl.BlockSpec((1, d), lambda b, i: (0, 0)),
            pl.BlockSpec((d, kd), lambda b, i: (0, 0)),
            pl.BlockSpec((1, kd), lambda b, i: (0, 0)),
            pl.BlockSpec((256, 256), lambda b, i: (0, 0)),
        ],
        out_specs=pl.BlockSpec((1, tm, kd), lambda b, i: (b, i, 0)),
        out_shape=SDS((nb, t, kd), BF16),
        compiler_params=_cparams(("parallel", "parallel")),
        name="q_proj",
    )(x3, g, w_q, qn_t, gmat)


def _qt_body(x_ref, g_ref, wt_ref, qn_ref, q_ref, *, nha, ahd, scale):
    u = _rms(x_ref[0], g_ref[...]).astype(BF16)
    qt = _row_group_norm(_dot_nt(wt_ref[...], u), ahd) * (qn_ref[...] * scale)
    rows = lax.broadcasted_iota(jnp.int32, (LANES, qt.shape[1]), 0)
    for h in range(nha):
        blk = qt[h * LANES:(h + 1) * LANES, :]
        q_ref[0, h, 0] = jnp.where(rows < ahd, blk, 0.0).astype(BF16)
        q_ref[0, h, 1] = jnp.where(rows >= ahd, blk, 0.0).astype(BF16)


def _qproj_t(x3, g, wq_t, qn_col, tm, nha, ahd, scale):
    nb, t, d = x3.shape
    kd = nha * LANES
    return pl.pallas_call(
        functools.partial(_qt_body, nha=nha, ahd=ahd, scale=scale),
        grid=(nb, t // tm),
        in_specs=[
            pl.BlockSpec((1, tm, d), lambda b, i: (b, i, 0)),
            pl.BlockSpec((1, d), lambda b, i: (0, 0)),
            pl.BlockSpec((kd, d), lambda b, i: (0, 0)),
            pl.BlockSpec((kd, 1), lambda b, i: (0, 0)),
        ],
        out_specs=pl.BlockSpec((1, nha, 2, LANES, tm), lambda b, i: (b, 0, 0, 0, i)),
        out_shape=SDS((nb, nha, 2, LANES, t), BF16),
        compiler_params=_cparams(("parallel", "parallel")),
        name="q_proj_t",
    )(x3, g, wq_t, qn_col)


def _lam(l1, k1, l2, k2, lam_init):
    return (jnp.exp(jnp.sum(l1 * k1, axis=-1, keepdims=True))
            - jnp.exp(jnp.sum(l2 * k2, axis=-1, keepdims=True)) + lam_init)


def _attn_body(it_ref, jt_ref, qt_ref, k_ref, vt_ref, l1_ref, k1_ref, l2_ref, k2_ref, o_ref,
               m_scr, l_scr, acc_scr, st_scr, mx_scr, *, tq, tk, nha, lam_init):
    t = pl.program_id(1)
    i = it_ref[t]
    j = jt_ref[t]

    @pl.when(j == 0)
    def _():
        m_scr[...] = jnp.full_like(m_scr, NEG)
        l_scr[...] = jnp.zeros_like(l_scr)
        acc_scr[...] = jnp.zeros_like(acc_scr)

    def scores(h, c, slot, bias):
        st = _dot(k_ref[0, h], qt_ref[0, h, c])
        if bias is not None:
            st = st + bias
        st_scr[slot] = st
        mx_scr[slot] = jnp.max(st, axis=0, keepdims=True)

    def accumulate(h, c, slot):
        m_prev = m_scr[h, c]
        m_new = jnp.maximum(m_prev, mx_scr[slot])
        alpha = jnp.exp2(m_prev - m_new)
        p = jnp.exp2(st_scr[slot] - m_new)
        l_scr[h, c] = alpha * l_scr[h, c] + jnp.sum(p, axis=0, keepdims=True)
        acc_scr[h, c] = alpha * acc_scr[h, c] + _dot(vt_ref[0, h], p.astype(BF16))
        m_scr[h, c] = m_new

    def run_heads(bias):
        scores(0, 0, 0, bias)

        def head(h, carry):
            scores(h, 1, 1, bias)
            accumulate(h, 0, 0)
            scores(jnp.minimum(h + 1, nha - 1), 0, 0, bias)
            accumulate(h, 1, 1)
            return carry

        lax.fori_loop(0, nha, head, 0, unroll=4)

    needs_mask = (j + 1) * tk - 1 > i * tq

    @pl.when(needs_mask)
    def _():
        kpos = j * tk + lax.broadcasted_iota(jnp.int32, (tk, tq), 0)
        qpos = i * tq + lax.broadcasted_iota(jnp.int32, (tk, tq), 1)
        run_heads(jnp.where(kpos <= qpos, 0.0, NEG))

    @pl.when(jnp.logical_not(needs_mask))
    def _():
        run_heads(None)

    @pl.when((j + 1) * tk >= (i + 1) * tq)
    def _():
        lam = _lam(l1_ref[...], k1_ref[...], l2_ref[...], k2_ref[...], lam_init)
        for h in range(nha):
            ot = (acc_scr[h, 0] * (1.0 / l_scr[h, 0])
                  - (lam * (1.0 / l_scr[h, 1])) * acc_scr[h, 1])
            o_ref[0, :, h * LANES:(h + 1) * LANES] = ot.T


def _attn_prompt(qt, kh, vht, l1, k1, l2, k2, tq, tk, hd, lam_init):
    nb, nha, _, _, t = qt.shape
    its, jts = [], []
    for i in range(t // tq):
        for j in range(((i + 1) * tq + tk - 1) // tk):
            its.append(i)
            jts.append(j)
    it = jnp.asarray(its, jnp.int32)
    jt = jnp.asarray(jts, jnp.int32)
    lspec = pl.BlockSpec((1, hd), lambda b, s, it, jt: (0, 0))
    grid_spec = pltpu.PrefetchScalarGridSpec(
        num_scalar_prefetch=2,
        grid=(nb, len(its)),
        in_specs=[
            pl.BlockSpec((1, nha, 2, LANES, tq), lambda b, s, it, jt: (b, 0, 0, 0, it[s])),
            pl.BlockSpec((1, nha, tk, LANES), lambda b, s, it, jt: (b, 0, jt[s], 0)),
            pl.BlockSpec((1, nha, LANES, tk), lambda b, s, it, jt: (b, 0, 0, jt[s])),
            lspec, lspec, lspec, lspec,
        ],
        out_specs=pl.BlockSpec((1, tq, nha * LANES), lambda b, s, it, jt: (b, it[s], 0)),
        scratch_shapes=[
            pltpu.VMEM((nha, 2, 1, tq), F32),
            pltpu.VMEM((nha, 2, 1, tq), F32),
            pltpu.VMEM((nha, 2, LANES, tq), F32),
            pltpu.VMEM((2, tk, tq), F32),
            pltpu.VMEM((2, 1, tq), F32),
        ],
    )
    return pl.pallas_call(
        functools.partial(_attn_body, tq=tq, tk=tk, nha=nha, lam_init=lam_init),
        grid_spec=grid_spec,
        out_shape=SDS((nb, t, nha * LANES), F32),
        compiler_params=_cparams(("parallel", "arbitrary")),
        name="attn_prompt",
    )(it, jt, qt, kh, vht, l1, k1, l2, k2)


def _decode_body(pt_ref, q_ref, kn_ref, vn_ref, l1_ref, k1_ref, l2_ref, k2_ref, *rest,
                 pps, tn, nha, hd, lam_init):
    k_refs = rest[:pps]
    v_refs = rest[pps:2 * pps]
    o_ref = rest[2 * pps]
    qb, m_scr, l_scr, acc_scr = rest[2 * pps + 1:]
    b_idx = pl.program_id(0)
    j = pl.program_id(1)
    nmap = 2 * nha
    nrow = tn * nmap
    kd = nha * LANES
    page = k_refs[0].shape[2]

    def online(sc, vals):
        n = len(vals)
        m_prev = m_scr[...]
        blk_max = sc[:, 0:LANES]
        for b in range(1, n):
            blk_max = jnp.maximum(blk_max, sc[:, b * LANES:(b + 1) * LANES])
        m_new = jnp.maximum(m_prev, jnp.max(blk_max, axis=-1, keepdims=True))
        alpha = jnp.exp(m_prev - m_new)
        p = jnp.exp(sc - pltpu.repeat(m_new, n, axis=1))
        psum = p[:, 0:LANES]
        for b in range(1, n):
            psum = psum + p[:, b * LANES:(b + 1) * LANES]
        l_scr[...] = alpha * l_scr[...] + psum
        pv = _dot(p[:, 0:LANES].astype(BF16), vals[0])
        for b in range(1, n):
            pv = pv + _dot(p[:, b * LANES:(b + 1) * LANES].astype(BF16), vals[b])
        acc_scr[...] = pltpu.repeat(alpha, kd // LANES, axis=1) * acc_scr[...] + pv
        m_scr[...] = m_new

    def page_values(v_ref):
        return jnp.concatenate(
            [v_ref[0, pl.ds(h, page, stride=nha), :] for h in range(nha)], axis=1).astype(BF16)

    @pl.when(j == 0)
    def _():
        rid = lax.broadcasted_iota(jnp.int32, (nmap, kd), 0)
        lid = lax.broadcasted_iota(jnp.int32, (nmap, kd), 1)
        sel = (lid >= rid * hd) & (lid < (rid + 1) * hd)
        qf = q_ref[0].astype(F32)
        for tt in range(tn):
            rowq = jnp.broadcast_to(qf[tt:tt + 1, :], (nmap, kd))
            qb[tt * nmap:(tt + 1) * nmap, :] = jnp.where(sel, rowq, 0.0).astype(BF16)
        m_scr[...] = jnp.full_like(m_scr, NEG)
        l_scr[...] = jnp.zeros_like(l_scr)
        acc_scr[...] = jnp.zeros_like(acc_scr)
        nnew = kn_ref.shape[2]
        qrow = lax.broadcasted_iota(jnp.int32, (nrow, LANES), 0)
        ktok = lax.broadcasted_iota(jnp.int32, (nrow, LANES), 1) - b_idx * tn
        for blk in range(nnew // LANES):
            sc = _dot(qb[...], kn_ref[0, :, blk * LANES:(blk + 1) * LANES].astype(BF16))
            kt = ktok - blk * LANES
            sc = jnp.where((kt >= 0) & (kt * nmap <= qrow), sc, NEG)
            online(sc, [vn_ref[0, blk * LANES:(blk + 1) * LANES, :].astype(BF16)])

    scs = [_dot(qb[...], k_refs[b][0].astype(BF16)) for b in range(pps)]
    online(jnp.concatenate(scs, axis=1), [page_values(v_refs[b]) for b in range(pps)])

    @pl.when(j == pl.num_programs(1) - 1)
    def _():
        lam = _lam(l1_ref[...], k1_ref[...], l2_ref[...], k2_ref[...], lam_init)
        inv_l = 1.0 / jnp.sum(l_scr[...], axis=-1, keepdims=True)
        rid = lax.broadcasted_iota(jnp.int32, (nmap, 1), 0)
        rid2 = lax.broadcasted_iota(jnp.int32, (nmap, kd), 0)
        lid2 = lax.broadcasted_iota(jnp.int32, (nmap, kd), 1)
        own = (lid2 >= (rid2 >> 1) * LANES) & (lid2 < ((rid2 >> 1) + 1) * LANES)
        for tt in range(tn):
            il = inv_l[tt * nmap:(tt + 1) * nmap, :]
            coef = jnp.where((rid & 1) == 0, il, -lam * il)
            w = jnp.where(own, acc_scr[tt * nmap:(tt + 1) * nmap, :] * coef, 0.0)
            o_ref[0, tt:tt + 1, :] = jnp.sum(w, axis=0, keepdims=True)


def _attn_sample(page_table, q3, kt_new, v_new, cache_kt, cache_v2, l1, k1, l2, k2, pps, hd, lam_init):
    nb, tn, kd = q3.shape
    nha = kd // LANES
    n_pages = page_table.shape[1]
    page = cache_kt.shape[2]
    nnew = kt_new.shape[2]
    nrow = tn * 2 * nha
    assert page == LANES and nnew == LANES and nb * tn == nnew
    pt_flat = page_table.reshape(-1)

    def k_spec(b_off):
        return pl.BlockSpec((1, kd, page),
                            lambda b, j, pt: (pt[b * n_pages + j * pps + b_off], 0, 0))

    def v_spec(b_off):
        return pl.BlockSpec((1, page * nha, LANES),
                            lambda b, j, pt: (pt[b * n_pages + j * pps + b_off], 0, 0))

    lspec = pl.BlockSpec((1, hd), lambda b, j, pt: (0, 0))
    grid_spec = pltpu.PrefetchScalarGridSpec(
        num_scalar_prefetch=1,
        grid=(nb, n_pages // pps),
        in_specs=[
            pl.BlockSpec((1, tn, kd), lambda b, j, pt: (b, 0, 0)),
            pl.BlockSpec((1, kd, nnew), lambda b, j, pt: (0, 0, 0)),
            pl.BlockSpec((1, nnew, kd), lambda b, j, pt: (0, 0, 0)),
            lspec, lspec, lspec, lspec]
        + [k_spec(b) for b in range(pps)] + [v_spec(b) for b in range(pps)],
        out_specs=pl.BlockSpec((1, tn, kd), lambda b, j, pt: (b, 0, 0)),
        scratch_shapes=[
            pltpu.VMEM((nrow, kd), BF16),
            pltpu.VMEM((nrow, LANES), F32),
            pltpu.VMEM((nrow, LANES), F32),
            pltpu.VMEM((nrow, kd), F32),
        ],
    )
    return pl.pallas_call(
        functools.partial(_decode_body, pps=pps, tn=tn, nha=nha, hd=hd, lam_init=lam_init),
        grid_spec=grid_spec,
        out_shape=SDS((nb, tn, kd), F32),
        compiler_params=_cparams(("parallel", "arbitrary")),
        name="attn_sample",
    )(pt_flat, q3, kt_new, v_new, l1, k1, l2, k2, *([cache_kt] * pps), *([cache_v2] * pps))


def _attn_out_body(o_ref, x_ref, sw_ref, w_ref, y_ref, *, nha, post_scale):
    o = o_ref[...]
    parts = []
    for h in range(nha):
        blk = o[:, h * LANES:(h + 1) * LANES]
        ms = jnp.mean(blk * blk, axis=-1, keepdims=True)
        parts.append(blk * lax.rsqrt(ms + EPS))
    on = jnp.concatenate(parts, axis=1) * sw_ref[...] * post_scale
    y_ref[...] = x_ref[...] + _dot(on.astype(BF16), w_ref[...])


def _attn_out(o2, x2, sw_t, w_o, tm, nha, post_scale):
    m, d = x2.shape
    vd = o2.shape[1]
    return pl.pallas_call(
        functools.partial(_attn_out_body, nha=nha, post_scale=post_scale),
        grid=(m // tm,),
        in_specs=[
            pl.BlockSpec((tm, vd), lambda i: (i, 0)),
            pl.BlockSpec((tm, d), lambda i: (i, 0)),
            pl.BlockSpec((1, vd), lambda i: (0, 0)),
            pl.BlockSpec((vd, d), lambda i: (0, 0)),
        ],
        out_specs=pl.BlockSpec((tm, d), lambda i: (i, 0)),
        out_shape=SDS((m, d), F32),
        compiler_params=_cparams(("parallel",)),
        name="attn_out",
    )(o2, x2, sw_t, w_o)


def kernel(x_prompt, x_sample, state_ssm, state_conv, cache_k, cache_v, page_table, norm_a, w_in_a, conv_w, conv_b, dt_bias, a_log, d_skip, gnorm_w, w_out_a, kv_norm, w_kv, k_norm, norm_b, w_q, q_norm, lam_q1, lam_k1, lam_q2, lam_k2, subln, w_o, norm_m, w_up, w_down):
    n_a = norm_a.shape[0]
    n_b = norm_b.shape[0]
    d = x_prompt.shape[-1]
    nh, hd, ds = state_ssm.shape[2], state_ssm.shape[3], state_ssm.shape[4]
    di = nh * hd
    cdim = state_conv.shape[-1]
    ng = (cdim - di) // (2 * ds)
    nha, ahd = cache_k.shape[2], cache_k.shape[4]
    kd = nha * 2 * ahd
    page = cache_k.shape[1]
    assert 2 * ahd == LANES and cache_v.shape[3] == LANES and nh <= LANES

    emat = (jnp.arange(LANES)[:, None] == (jnp.arange(di)[None, :] // hd)).astype(BF16)
    emat = jnp.concatenate([emat, emat], axis=0)
    gidx = jnp.arange(256) // ahd
    gmat = jnp.where(gidx[:, None] == gidx[None, :], 1.0 / ahd, 0.0).astype(BF16)

    def row(v):
        return v.reshape(1, -1).astype(F32)

    def pad_lanes(v):
        return jnp.pad(v.astype(F32), (0, LANES - v.shape[0])).reshape(1, LANES)

    n_in = di + cdim + nh
    tn_in = -(-n_in // (2 * LANES)) * LANES
    w_in_b = [jnp.pad(w_in_a[l], ((0, 0), (0, 2 * tn_in - n_in))).astype(BF16) for l in range(n_a)]
    w_out_b = w_out_a.astype(BF16)
    wk_t_b = w_kv[:, :kd].T.astype(BF16)
    wv_b = w_kv[:, kd:].astype(BF16)
    w_q_b = w_q.astype(BF16)
    wq_t_b = jnp.swapaxes(w_q, 1, 2).astype(BF16)
    w_o_b = w_o.astype(BF16)
    w_up_b = w_up.astype(BF16)
    w_down_b = w_down.astype(BF16)
    scale = ahd ** -0.5

    def trunk(x3, conv0, ssm0, attend, seq_major):
        nb, t, _ = x3.shape
        m = nb * t
        tm = min(512, m)
        x2 = x3.reshape(m, d)
        ab, at = (nb, t) if seq_major else (1, m)
        tma = min(512, at)
        convs, ssms = [], []
        k_new = v_new = kv_ctx = None
        for l in range(n_a + n_b):
            if l < n_a:
                zxbc = _inproj(x2, row(norm_a[l]), w_in_b[l], tm, tn_in)
                y3, hn, cn = _ssd(
                    zxbc.reshape(nb, t, 2 * tn_in), conv0[l], ssm0[l].reshape(nb, di, ds),
                    conv_w[l], row(conv_b[l]), pad_lanes(dt_bias[l]), pad_lanes(a_log[l]),
                    row(jnp.repeat(d_skip[l], hd)), emat, nh=nh, hd=hd, ng=ng, ds=ds)
                convs.append(cn)
                ssms.append(hn.reshape(nb, nh, hd, ds))
                tmp = min(256, m)
                if t % tmp:
                    y3 = jnp.transpose(y3, (0, 2, 1, 3)).reshape(m, di)
                x2 = _post(y3, zxbc, x2, row(gnorm_w[l]), w_out_b[l], tmp, ng)
            else:
                jb = l - n_a
                if l == n_a:
                    kv_ctx = _kv(x2.reshape(ab, at, d), row(kv_norm), wk_t_b, wv_b,
                                 jnp.tile(k_norm, 2 * nha).reshape(kd, 1).astype(F32), tma, nha, ahd, seq_major)
                    k_new = jnp.transpose(kv_ctx[0].reshape(ab, nha, 2, ahd, at), (0, 4, 1, 2, 3))
                    k_new = k_new.reshape(nb, t, nha, 2, ahd)
                    v_new = kv_ctx[1].reshape(nb, t, nha, LANES)
                lam_init = 0.8 - 0.6 * math.exp(-0.3 * l)
                lams = (row(lam_q1[jb]), row(lam_k1[jb]), row(lam_q2[jb]), row(lam_k2[jb]))
                qn_t = jnp.tile(q_norm[jb], 2 * nha).astype(F32)
                if seq_major:
                    qout = _qproj_t(x2.reshape(ab, at, d), row(norm_b[jb]), wq_t_b[jb],
                                    qn_t.reshape(kd, 1), tma, nha, ahd, scale * LOG2E)
                else:
                    qout = _qproj(x2.reshape(ab, at, d), row(norm_b[jb]), w_q_b[jb],
                                  qn_t.reshape(1, kd), gmat, tma, nha, scale)
                o3 = attend(qout, kv_ctx, lams, lam_init, nb, t)
                x2 = _attn_out(o3.reshape(m, kd), x2, row(jnp.tile(subln[jb], nha)), w_o_b[jb],
                               tm, nha, 1.0 - lam_init)
            x2 = _mlp(x2, row(norm_m[l]), w_up_b[l], w_down_b[l], min(1024, m), 1024)
        return x2.reshape(nb, t, d), jnp.stack(ssms), jnp.stack(convs), k_new, v_new

    def attend_prompt(qh, kv_ctx, lams, lam_init, nb, t):
        return _attn_prompt(qh, kv_ctx[2], kv_ctx[3], *lams, 512, 512, ahd, lam_init)

    n_pool = cache_k.shape[0]
    cache_kt = jnp.transpose(cache_k, (0, 2, 3, 4, 1)).reshape(n_pool, kd, page)
    cache_v2 = cache_v.reshape(n_pool, page * nha, LANES)

    def attend_sample(q3, kv_ctx, lams, lam_init, nb, t):
        return _attn_sample(page_table, q3.reshape(nb, t, kd), kv_ctx[0], kv_ctx[1],
                            cache_kt, cache_v2, *lams, 8, ahd, lam_init)

    bp = x_prompt.shape[0]
    conv0_p = jnp.zeros((n_a, bp, CONV_W - 1, cdim), F32)
    ssm0_p = jnp.zeros((n_a, bp, nh, hd, ds), F32)
    y_p, ssm_p, conv_p, k_p, v_p = trunk(x_prompt, conv0_p, ssm0_p, attend_prompt, True)
    y_s, ssm_s, conv_s, k_s, v_s = trunk(x_sample, state_conv, state_ssm, attend_sample, False)
    return (y_p, y_s, ssm_p, conv_p, k_p, v_p, ssm_s, conv_s, k_s, v_s)
```

```python
import functools
import math

import jax
import jax.numpy as jnp
from jax import lax
from jax.experimental import pallas as pl
from jax.experimental.pallas import tpu as pltpu

F32 = jnp.float32
BF16 = jnp.bfloat16
EPS = 1e-6
LANES = 128
CHUNK = 128
CONV_W = 4
VMEM_LIMIT = 48 * 1024 * 1024
MLP_VMEM_LIMIT = 56 * 1024 * 1024
NEG = -1e30
LOG2E = math.log2(math.e)
SDS = jax.ShapeDtypeStruct


def _cparams(sem):
    return pltpu.CompilerParams(dimension_semantics=sem, vmem_limit_bytes=VMEM_LIMIT)


def _rms(x, g):
    ms = jnp.mean(x * x, axis=-1, keepdims=True)
    return x * lax.rsqrt(ms + EPS) * g


def _silu(x):
    return x * (1.0 / (1.0 + jnp.exp(-x)))


def _softplus(x):
    e = jnp.exp(-jnp.abs(x))
    u = 1.0 + e
    log1p_e = jnp.where(u == 1.0, e, jnp.log(u) * (e / (u - 1.0)))
    return jnp.maximum(x, 0.0) + log1p_e


def _dot(a, b):
    return jnp.dot(a, b, preferred_element_type=F32)


def _dot_nt(a, b):
    return lax.dot_general(a, b, (((1,), (1,)), ((), ())), preferred_element_type=F32)


def _split_bf16(x):
    hi = x.astype(BF16)
    lo = (x - hi.astype(F32)).astype(BF16)
    return hi, lo


def _group_norm64(x, gmat):
    parts = []
    for c in range(x.shape[1] // 256):
        blk = x[:, c * 256:(c + 1) * 256]
        ms = _dot((blk * blk).astype(BF16), gmat)
        parts.append(blk * lax.rsqrt(ms + EPS))
    return jnp.concatenate(parts, axis=1)


def _inproj_body(x_ref, g_ref, w_ref, o_ref):
    u = _rms(x_ref[...], g_ref[...]).astype(BF16)
    o_ref[...] = _dot(u, w_ref[0])


def _inproj(x2, g, w_all, layer, tm, tn):
    m, d = x2.shape
    n = w_all.shape[2]
    return pl.pallas_call(
        _inproj_body,
        grid=(n // tn, m // tm),
        in_specs=[
            pl.BlockSpec((tm, d), lambda j, i: (i, 0)),
            pl.BlockSpec((1, d), lambda j, i: (0, 0)),
            pl.BlockSpec((1, d, tn), lambda j, i: (layer, 0, j)),
        ],
        out_specs=pl.BlockSpec((tm, tn), lambda j, i: (i, j)),
        out_shape=SDS((m, n), F32),
        compiler_params=_cparams(("parallel", "parallel")),
        name="inproj",
    )(x2, g, w_all)


def _ssd_body(xa_ref, xb_ref, dt_ref, conv0_ref, h0_ref, cw_ref, cb_ref, dtb_ref, alog_ref,
              dsk_ref, e_ref, y_ref, hn_ref, convn_ref, xpad, dtpad, hst, wsp,
              *, ts, nh, hd, ng, ds):
    q = CHUNK
    nph = q // 8
    di = nh * hd
    gn = ng * ds
    hpg = nh // ng
    gw = hpg * hd
    nxs = di // LANES
    ncs = xpad.shape[0]
    pad = CONV_W - 1
    s = pl.program_id(1)

    @pl.when(s == 0)
    def _():
        xpad[...] = jnp.zeros_like(xpad)
        dtpad[...] = jnp.zeros_like(dtpad)
        for c in range(ncs):
            xpad[c, 8 - pad:8, :] = conv0_ref[0, 0, :, c * LANES:(c + 1) * LANES]
        hst[...] = h0_ref[0, 0]

    for c in range(ncs):
        src = xa_ref if c < nxs else xb_ref
        cc0 = (c % nxs) * LANES
        xpad[c, 8:8 + ts, :] = src[0, :, cc0:cc0 + LANES]
        convn_ref[0, :, c * LANES:(c + 1) * LANES] = xpad[c, 8 + ts - pad:8 + ts, :]
    dtpad[0:ts, :] = dt_ref[0]

    def phase_rows(ref2d, r0):
        return jnp.concatenate([ref2d[pl.ds(r0 + ph, nph, stride=8), :] for ph in range(8)], axis=0)

    def conv_silu(c0, width):
        slabs = []
        for c in range(c0 // LANES, (c0 + width) // LANES):
            acc = cb_ref[:, c * LANES:(c + 1) * LANES]
            for k in range(CONV_W):
                acc = acc + cw_ref[k:k + 1, c * LANES:(c + 1) * LANES] * phase_rows(xpad.at[c], 8 - pad + k)
            slabs.append(_silu(acc))
        return slabs[0] if len(slabs) == 1 else jnp.concatenate(slabs, axis=1)

    def real_time(pos):
        return ((pos & (nph - 1)) << 3) | (pos >> (nph.bit_length() - 1))

    def front(g):
        xg = conv_silu(g * gw, gw)
        bb = conv_silu(di + g * ds, ds).astype(BF16)
        cc = conv_silu(di + gn + g * ds, ds).astype(BF16)
        cbm = _dot_nt(cc, bb)
        hg = hst[g * gw:(g + 1) * gw, :]
        yoff = _dot_nt(cc, hg.astype(BF16))
        e_g = e_ref[:, g * gw:(g + 1) * gw]
        w1g = _dot(wsp[0], e_g)
        w2g = _dot(wsp[1], e_g)
        return xg, bb, cbm, hg, yoff, w1g, w2g

    dt = _softplus(phase_rows(dtpad, 0) + dtb_ref[...])
    if ts < q:
        dt = jnp.where(real_time(lax.broadcasted_iota(jnp.int32, (q, LANES), 0)) < ts, dt, 0.0)
    a = dt * (-jnp.exp(alog_ref[...]))
    causal = (real_time(lax.broadcasted_iota(jnp.int32, (q, q), 0))
              >= real_time(lax.broadcasted_iota(jnp.int32, (q, q), 1)))
    tril = jnp.where(causal, 1.0, 0.0).astype(BF16)

    a_hi, a_lo = _split_bf16(a)
    acs = _dot(jnp.concatenate([tril, tril], axis=1),
               jnp.concatenate([a_hi, a_lo], axis=0))
    acs2 = acs * LOG2E
    acs2_t = acs2.T
    dt_t = dt.T
    acs_last = acs[q - 1:q, :]
    wsp[0] = jnp.concatenate(_split_bf16(dt * jnp.exp(acs_last - acs)), axis=1)
    wsp[1] = jnp.concatenate(_split_bf16(jnp.exp(acs)), axis=1)
    cd = jnp.exp(acs_last)
    lane_g = lax.broadcasted_iota(jnp.int32, (1, gw), 1)

    nxt = front(0)
    for g in range(ng):
        xg, bb, cbm, hg, yoff, w1g, w2g = nxt
        if g + 1 < ng:
            nxt = front(g + 1)
        xgb = xg.astype(BF16)
        yd = jnp.zeros((q, gw), F32)
        for r in range(hpg):
            h = g * hpg + r
            seg = acs2[:, h:h + 1] - acs2_t[h:h + 1, :]
            dec = jnp.exp2(jnp.where(causal, seg, NEG))
            mh = (cbm * dec * dt_t[h:h + 1, :]).astype(BF16)
            yd = jnp.where((lane_g >= r * hd) & (lane_g < (r + 1) * hd), _dot(mh, xgb), yd)
        yg = yd + yoff * w2g + dsk_ref[:, g * gw:(g + 1) * gw] * xg
        for jj in range(gw // LANES):
            slab = yg[:, jj * LANES:(jj + 1) * LANES]
            cy = g * (gw // LANES) + jj
            if ts == q:
                for ph in range(8):
                    y_ref[0, cy, pl.ds(ph, nph, stride=8), :] = slab[ph * nph:(ph + 1) * nph]
            else:
                for tt in range(ts):
                    pos = (tt % 8) * nph + tt // 8
                    y_ref[0, cy, tt:tt + 1, :] = slab[pos:pos + 1]
        st = _dot((xg * w1g).T.astype(BF16), bb)
        cdt = jnp.concatenate(
            [jnp.broadcast_to(cd[:, g * hpg + r:g * hpg + r + 1], (hd, ds)) for r in range(hpg)], axis=0)
        hst[g * gw:(g + 1) * gw, :] = hg * cdt + st

    for c in range(ncs):
        xpad[c, 8 - pad:8, :] = xpad[c, 8 + ts - pad:8 + ts, :]

    @pl.when(s == pl.num_programs(1) - 1)
    def _():
        hn_ref[0] = hst[...]


def _ssd(zxbc3, conv0, h0, layer, cw, cb, dtb, alog, dsk, emat, *, nh, hd, ng, ds):
    nb, length, _ = zxbc3.shape
    di = nh * hd
    cdim = di + 2 * ng * ds
    ts = min(CHUNK, length)
    ns = length // ts
    body = functools.partial(_ssd_body, ts=ts, nh=nh, hd=hd, ng=ng, ds=ds)
    assert cdim == 2 * di
    dt_blk = (di + cdim) // LANES
    return pl.pallas_call(
        body,
        grid=(nb, ns),
        in_specs=[
            pl.BlockSpec((1, ts, di), lambda b, s: (b, s, 1)),
            pl.BlockSpec((1, ts, di), lambda b, s: (b, s, 2)),
            pl.BlockSpec((1, ts, LANES), lambda b, s: (b, s, dt_blk)),
            pl.BlockSpec((1, 1, CONV_W - 1, cdim), lambda b, s: (layer, b, 0, 0)),
            pl.BlockSpec((1, 1, di, ds), lambda b, s: (layer, b, 0, 0)),
            pl.BlockSpec((CONV_W, cdim), lambda b, s: (0, 0)),
            pl.BlockSpec((1, cdim), lambda b, s: (0, 0)),
            pl.BlockSpec((1, LANES), lambda b, s: (0, 0)),
            pl.BlockSpec((1, LANES), lambda b, s: (0, 0)),
            pl.BlockSpec((1, di), lambda b, s: (0, 0)),
            pl.BlockSpec((2 * LANES, di), lambda b, s: (0, 0)),
        ],
        out_specs=[
            pl.BlockSpec((1, di // LANES, ts, LANES), lambda b, s: (b, 0, s, 0)),
            pl.BlockSpec((1, di, ds), lambda b, s: (b, 0, 0)),
            pl.BlockSpec((1, CONV_W - 1, cdim), lambda b, s: (b, 0, 0)),
        ],
        out_shape=[SDS((nb, di // LANES, length, LANES), F32), SDS((nb, di, ds), F32),
                   SDS((nb, CONV_W - 1, cdim), F32)],
        scratch_shapes=[
            pltpu.VMEM((cdim // LANES, 8 + CHUNK, LANES), F32),
            pltpu.VMEM((CHUNK, LANES), F32),
            pltpu.VMEM((di, ds), F32),
            pltpu.VMEM((2, CHUNK, 2 * LANES), BF16),
        ],
        compiler_params=_cparams(("parallel", "arbitrary")),
        name="ssd",
    )(zxbc3, zxbc3, zxbc3, conv0, h0, cw, cb, dtb, alog, dsk, emat)


def _post_body(y_ref, z_ref, x_ref, gw_ref, w_ref, o_ref, *, ng):
    if len(y_ref.shape) == 4:
        y = jnp.concatenate([y_ref[0, c] for c in range(y_ref.shape[1])], axis=1)
    else:
        y = y_ref[...]
    yz = y * _silu(z_ref[...])
    width = yz.shape[1] // ng
    parts = []
    for g in range(ng):
        blk = yz[:, g * width:(g + 1) * width]
        ms = jnp.mean(blk * blk, axis=-1, keepdims=True)
        parts.append(blk * lax.rsqrt(ms + EPS))
    yn = jnp.concatenate(parts, axis=1) * gw_ref[...]
    o_ref[...] = x_ref[...] + _dot(yn.astype(BF16), w_ref[0].astype(BF16))


def _post(y, zxbc2, x2, gw, w_out, layer, tm, ng):
    m, d = x2.shape
    if y.ndim == 4:
        nslab, length = y.shape[1], y.shape[2]
        di = nslab * LANES
        per_seq = length // tm
        y_spec = pl.BlockSpec((1, nslab, tm, LANES), lambda i: (i // per_seq, 0, i % per_seq, 0))
    else:
        di = y.shape[1]
        y_spec = pl.BlockSpec((tm, di), lambda i: (i, 0))
    return pl.pallas_call(
        functools.partial(_post_body, ng=ng),
        grid=(m // tm,),
        in_specs=[
            y_spec,
            pl.BlockSpec((tm, di), lambda i: (i, 0)),
            pl.BlockSpec((tm, d), lambda i: (i, 0)),
            pl.BlockSpec((1, di), lambda i: (0, 0)),
            pl.BlockSpec((1, di, d), lambda i: (layer, 0, 0)),
        ],
        out_specs=pl.BlockSpec((tm, d), lambda i: (i, 0)),
        out_shape=SDS((m, d), F32),
        compiler_params=_cparams(("parallel",)),
        name="post_mamba",
    )(y, zxbc2, x2, gw, w_out)


def _mlp_body(x_ref, g_ref, wu_ref, wd_ref, o_ref, u_scr):
    c = pl.program_id(1)

    @pl.when(c == 0)
    def _():
        u_scr[...] = _rms(x_ref[...], g_ref[...]).astype(BF16)

    u = u_scr[...]
    half = wu_ref.shape[2] // 2
    hs = [jnp.maximum(_dot(u, wu_ref[0, :, k * half:(k + 1) * half].astype(BF16)), 0.0) for k in range(2)]
    part = _dot((hs[0] * hs[0]).astype(BF16), wd_ref[0, 0:half, :].astype(BF16))
    part = part + _dot((hs[1] * hs[1]).astype(BF16), wd_ref[0, half:2 * half, :].astype(BF16))

    @pl.when(c == 0)
    def _():
        o_ref[...] = x_ref[...] + part

    @pl.when(c != 0)
    def _():
        o_ref[...] += part


def _mlp(x2, g, w_up, w_down, layer, tm, tf):
    m, d = x2.shape
    ff = w_up.shape[2]
    return pl.pallas_call(
        _mlp_body,
        grid=(m // tm, ff // tf),
        in_specs=[
            pl.BlockSpec((tm, d), lambda i, c: (i, 0)),
            pl.BlockSpec((1, d), lambda i, c: (0, 0)),
            pl.BlockSpec((1, d, tf), lambda i, c: (layer, 0, c)),
            pl.BlockSpec((1, tf, d), lambda i, c: (layer, c, 0)),
        ],
        out_specs=pl.BlockSpec((tm, d), lambda i, c: (i, 0)),
        out_shape=SDS((m, d), F32),
        scratch_shapes=[pltpu.VMEM((tm, d), BF16)],
        compiler_params=pltpu.CompilerParams(dimension_semantics=("parallel", "arbitrary"),
                                             vmem_limit_bytes=MLP_VMEM_LIMIT),
        name="mlp",
    )(x2, g, w_up, w_down)


def _row_group_norm(xt, width):
    n, tm = xt.shape
    x3 = xt.reshape(n // width, width, tm)
    ms = jnp.mean(x3 * x3, axis=1, keepdims=True)
    return (x3 * lax.rsqrt(ms + EPS)).reshape(n, tm)


def _kv_body(x_ref, g_ref, wkt_ref, wv_ref, kn_ref, kt_ref, v_ref, *head_refs, nha, ahd):
    u = _rms(x_ref[0], g_ref[...]).astype(BF16)
    kt = _row_group_norm(_dot_nt(wkt_ref[...], u), ahd) * kn_ref[...]
    v = _dot(u, wv_ref[...])
    kt_ref[0] = kt
    v_ref[0] = v
    if head_refs:
        kh_ref, vht_ref = head_refs
        k = kt.T
        for h in range(nha):
            kh_ref[0, h] = k[:, h * LANES:(h + 1) * LANES].astype(BF16)
            vht_ref[0, h] = v[:, h * LANES:(h + 1) * LANES].T.astype(BF16)


def _kv(x3, g, wk_t, wv, kn_col, tm, nha, ahd, head_major):
    nb, t, d = x3.shape
    kd = nha * LANES
    out_specs = [
        pl.BlockSpec((1, kd, tm), lambda b, i: (b, 0, i)),
        pl.BlockSpec((1, tm, kd), lambda b, i: (b, i, 0)),
    ]
    out_shape = [SDS((nb, kd, t), F32), SDS((nb, t, kd), F32)]
    if head_major:
        out_specs += [
            pl.BlockSpec((1, nha, tm, LANES), lambda b, i: (b, 0, i, 0)),
            pl.BlockSpec((1, nha, LANES, tm), lambda b, i: (b, 0, 0, i)),
        ]
        out_shape += [SDS((nb, nha, t, LANES), BF16), SDS((nb, nha, LANES, t), BF16)]
    return pl.pallas_call(
        functools.partial(_kv_body, nha=nha, ahd=ahd),
        grid=(nb, t // tm),
        in_specs=[
            pl.BlockSpec((1, tm, d), lambda b, i: (b, i, 0)),
            pl.BlockSpec((1, d), lambda b, i: (0, 0)),
            pl.BlockSpec((kd, d), lambda b, i: (0, 0)),
            pl.BlockSpec((d, kd), lambda b, i: (0, 0)),
            pl.BlockSpec((kd, 1), lambda b, i: (0, 0)),
        ],
        out_specs=out_specs,
        out_shape=out_shape,
        compiler_params=_cparams(("parallel", "parallel")),
        name="kv_proj",
    )(x3, g, wk_t, wv, kn_col)


def _q_body(x_ref, g_ref, w_ref, qn_ref, gm_ref, q_ref, *, scale):
    u = _rms(x_ref[0], g_ref[...]).astype(BF16)
    qn = _group_norm64(_dot(u, w_ref[0].astype(BF16)), gm_ref[...]) * qn_ref[...] * scale
    q_ref[0] = qn.astype(BF16)


def _qproj(x3, g, w_q, layer, qn_t, gmat, tm, nha, scale):
    nb, t, d = x3.shape
    kd = nha * LANES
    return pl.pallas_call(
        functools.partial(_q_body, scale=scale),
        grid=(nb, t // tm),
        in_specs=[
            pl.BlockSpec((1, tm, d), lambda b, i: (b, i, 0)),
            pl.BlockSpec((1, d), lambda b, i: (0, 0)),
            pl.BlockSpec((1, d, kd), lambda b, i: (layer, 0, 0)),
            pl.BlockSpec((1, kd), lambda b, i: (0, 0)),
            pl.BlockSpec((256, 256), lambda b, i: (0, 0)),
        ],
        out_specs=pl.BlockSpec((1, tm, kd), lambda b, i: (b, i, 0)),
        out_shape=SDS((nb, t, kd), BF16),
        compiler_params=_cparams(("parallel", "parallel")),
        name="q_proj",
    )(x3, g, w_q, qn_t, gmat)


def _qt_body(x_ref, g_ref, wt_ref, qn_ref, q_ref, *, nha, ahd, scale):
    u = _rms(x_ref[0], g_ref[...]).astype(BF16)
    qt = _row_group_norm(_dot_nt(wt_ref[...], u), ahd) * (qn_ref[...] * scale)
    rows = lax.broadcasted_iota(jnp.int32, (LANES, qt.shape[1]), 0)
    for h in range(nha):
        blk = qt[h * LANES:(h + 1) * LANES, :]
        q_ref[0, h, 0] = jnp.where(rows < ahd, blk, 0.0).astype(BF16)
        q_ref[0, h, 1] = jnp.where(rows >= ahd, blk, 0.0).astype(BF16)


def _qproj_t(x3, g, wq_t, qn_col, tm, nha, ahd, scale):
    nb, t, d = x3.shape
    kd = nha * LANES
    return pl.pallas_call(
        functools.partial(_qt_body, nha=nha, ahd=ahd, scale=scale),
        grid=(nb, t // tm),
        in_specs=[
            pl.BlockSpec((1, tm, d), lambda b, i: (b, i, 0)),
            pl.BlockSpec((1, d), lambda b, i: (0, 0)),
            pl.BlockSpec((kd, d), lambda b, i: (0, 0)),
            pl.BlockSpec((kd, 1), lambda b, i: (0, 0)),
        ],
        out_specs=pl.BlockSpec((1, nha, 2, LANES, tm), lambda b, i: (b, 0, 0, 0, i)),
        out_shape=SDS((nb, nha, 2, LANES, t), BF16),
        compiler_params=_cparams(("parallel", "parallel")),
        name="q_proj_t",
    )(x3, g, wq_t, qn_col)


def _lam(l1, k1, l2, k2, lam_init):
    return (jnp.exp(jnp.sum(l1 * k1, axis=-1, keepdims=True))
            - jnp.exp(jnp.sum(l2 * k2, axis=-1, keepdims=True)) + lam_init)


def _attn_body(it_ref, jt_ref, qt_ref, k_ref, vt_ref, l1_ref, k1_ref, l2_ref, k2_ref, o_ref,
               m_scr, l_scr, acc_scr, st_scr, mx_scr, *, tq, tk, nha, lam_init):
    t = pl.program_id(1)
    i = it_ref[t]
    j = jt_ref[t]

    @pl.when(j == 0)
    def _():
        m_scr[...] = jnp.full_like(m_scr, NEG)
        l_scr[...] = jnp.zeros_like(l_scr)
        acc_scr[...] = jnp.zeros_like(acc_scr)

    def scores(h, c, slot, bias):
        st = _dot(k_ref[0, h], qt_ref[0, h, c])
        if bias is not None:
            st = st + bias
        st_scr[slot] = st
        mx_scr[slot] = jnp.max(st, axis=0, keepdims=True)

    def accumulate(h, c, slot):
        m_prev = m_scr[h, c]
        m_new = jnp.maximum(m_prev, mx_scr[slot])
        alpha = jnp.exp2(m_prev - m_new)
        p = jnp.exp2(st_scr[slot] - m_new)
        l_scr[h, c] = alpha * l_scr[h, c] + jnp.sum(p, axis=0, keepdims=True)
        acc_scr[h, c] = alpha * acc_scr[h, c] + _dot(vt_ref[0, h], p.astype(BF16))
        m_scr[h, c] = m_new

    def run_heads(bias):
        scores(0, 0, 0, bias)

        def head(h, carry):
            scores(h, 1, 1, bias)
            accumulate(h, 0, 0)
            scores(jnp.minimum(h + 1, nha - 1), 0, 0, bias)
            accumulate(h, 1, 1)
            return carry

        lax.fori_loop(0, nha, head, 0, unroll=4)

    needs_mask = (j + 1) * tk - 1 > i * tq

    @pl.when(needs_mask)
    def _():
        kpos = j * tk + lax.broadcasted_iota(jnp.int32, (tk, tq), 0)
        qpos = i * tq + lax.broadcasted_iota(jnp.int32, (tk, tq), 1)
        run_heads(jnp.where(kpos <= qpos, 0.0, NEG))

    @pl.when(jnp.logical_not(needs_mask))
    def _():
        run_heads(None)

    @pl.when((j + 1) * tk >= (i + 1) * tq)
    def _():
        lam = _lam(l1_ref[...], k1_ref[...], l2_ref[...], k2_ref[...], lam_init)
        for h in range(nha):
            ot = (acc_scr[h, 0] * (1.0 / l_scr[h, 0])
                  - (lam * (1.0 / l_scr[h, 1])) * acc_scr[h, 1])
            o_ref[0, :, h * LANES:(h + 1) * LANES] = ot.T


def _attn_prompt(qt, kh, vht, l1, k1, l2, k2, tq, tk, hd, lam_init):
    nb, nha, _, _, t = qt.shape
    its, jts = [], []
    for i in range(t // tq):
        for j in range(((i + 1) * tq + tk - 1) // tk):
            its.append(i)
            jts.append(j)
    it = jnp.asarray(its, jnp.int32)
    jt = jnp.asarray(jts, jnp.int32)
    lspec = pl.BlockSpec((1, hd), lambda b, s, it, jt: (0, 0))
    grid_spec = pltpu.PrefetchScalarGridSpec(
        num_scalar_prefetch=2,
        grid=(nb, len(its)),
        in_specs=[
            pl.BlockSpec((1, nha, 2, LANES, tq), lambda b, s, it, jt: (b, 0, 0, 0, it[s])),
            pl.BlockSpec((1, nha, tk, LANES), lambda b, s, it, jt: (b, 0, jt[s], 0)),
            pl.BlockSpec((1, nha, LANES, tk), lambda b, s, it, jt: (b, 0, 0, jt[s])),
            lspec, lspec, lspec, lspec,
        ],
        out_specs=pl.BlockSpec((1, tq, nha * LANES), lambda b, s, it, jt: (b, it[s], 0)),
        scratch_shapes=[
            pltpu.VMEM((nha, 2, 1, tq), F32),
            pltpu.VMEM((nha, 2, 1, tq), F32),
            pltpu.VMEM((nha, 2, LANES, tq), F32),
            pltpu.VMEM((2, tk, tq), F32),
            pltpu.VMEM((2, 1, tq), F32),
        ],
    )
    return pl.pallas_call(
        functools.partial(_attn_body, tq=tq, tk=tk, nha=nha, lam_init=lam_init),
        grid_spec=grid_spec,
        out_shape=SDS((nb, t, nha * LANES), F32),
        compiler_params=_cparams(("parallel", "arbitrary")),
        name="attn_prompt",
    )(it, jt, qt, kh, vht, l1, k1, l2, k2)


def _decode_body(pt_ref, q_ref, kn_ref, vn_ref, l1_ref, k1_ref, l2_ref, k2_ref, *rest,
                 pps, tn, nha, hd, lam_init):
    k_refs = rest[:pps]
    v_refs = rest[pps:2 * pps]
    o_ref = rest[2 * pps]
    qb, m_scr, l_scr, acc_scr = rest[2 * pps + 1:]
    b_idx = pl.program_id(0)
    j = pl.program_id(1)
    nmap = 2 * nha
    nrow = tn * nmap
    kd = nha * LANES
    page = k_refs[0].shape[2]

    def online(sc, vals):
        n = len(vals)
        m_prev = m_scr[...]
        blk_max = sc[:, 0:LANES]
        for b in range(1, n):
            blk_max = jnp.maximum(blk_max, sc[:, b * LANES:(b + 1) * LANES])
        m_new = jnp.maximum(m_prev, jnp.max(blk_max, axis=-1, keepdims=True))
        alpha = jnp.exp(m_prev - m_new)
        p = jnp.exp(sc - pltpu.repeat(m_new, n, axis=1))
        psum = p[:, 0:LANES]
        for b in range(1, n):
            psum = psum + p[:, b * LANES:(b + 1) * LANES]
        l_scr[...] = alpha * l_scr[...] + psum
        pv = _dot(p[:, 0:LANES].astype(BF16), vals[0])
        for b in range(1, n):
            pv = pv + _dot(p[:, b * LANES:(b + 1) * LANES].astype(BF16), vals[b])
        acc_scr[...] = pltpu.repeat(alpha, kd // LANES, axis=1) * acc_scr[...] + pv
        m_scr[...] = m_new

    def page_values(v_ref):
        return jnp.concatenate(
            [v_ref[0, pl.ds(h, page, stride=nha), :] for h in range(nha)], axis=1).astype(BF16)

    @pl.when(j == 0)
    def _():
        rid = lax.broadcasted_iota(jnp.int32, (nmap, kd), 0)
        lid = lax.broadcasted_iota(jnp.int32, (nmap, kd), 1)
        sel = (lid >= rid * hd) & (lid < (rid + 1) * hd)
        qf = q_ref[0].astype(F32)
        for tt in range(tn):
            rowq = jnp.broadcast_to(qf[tt:tt + 1, :], (nmap, kd))
            qb[tt * nmap:(tt + 1) * nmap, :] = jnp.where(sel, rowq, 0.0).astype(BF16)
        m_scr[...] = jnp.full_like(m_scr, NEG)
        l_scr[...] = jnp.zeros_like(l_scr)
        acc_scr[...] = jnp.zeros_like(acc_scr)
        nnew = kn_ref.shape[2]
        qrow = lax.broadcasted_iota(jnp.int32, (nrow, LANES), 0)
        ktok = lax.broadcasted_iota(jnp.int32, (nrow, LANES), 1) - b_idx * tn
        for blk in range(nnew // LANES):
            sc = _dot(qb[...], kn_ref[0, :, blk * LANES:(blk + 1) * LANES].astype(BF16))
            kt = ktok - blk * LANES
            sc = jnp.where((kt >= 0) & (kt * nmap <= qrow), sc, NEG)
            online(sc, [vn_ref[0, blk * LANES:(blk + 1) * LANES, :].astype(BF16)])

    scs = [_dot(qb[...], k_refs[b][0].astype(BF16)) for b in range(pps)]
    online(jnp.concatenate(scs, axis=1), [page_values(v_refs[b]) for b in range(pps)])

    @pl.when(j == pl.num_programs(1) - 1)
    def _():
        lam = _lam(l1_ref[...], k1_ref[...], l2_ref[...], k2_ref[...], lam_init)
        inv_l = 1.0 / jnp.sum(l_scr[...], axis=-1, keepdims=True)
        rid = lax.broadcasted_iota(jnp.int32, (nmap, 1), 0)
        rid2 = lax.broadcasted_iota(jnp.int32, (nmap, kd), 0)
        lid2 = lax.broadcasted_iota(jnp.int32, (nmap, kd), 1)
        own = (lid2 >= (rid2 >> 1) * LANES) & (lid2 < ((rid2 >> 1) + 1) * LANES)
        for tt in range(tn):
            il = inv_l[tt * nmap:(tt + 1) * nmap, :]
            coef = jnp.where((rid & 1) == 0, il, -lam * il)
            w = jnp.where(own, acc_scr[tt * nmap:(tt + 1) * nmap, :] * coef, 0.0)
            o_ref[0, tt:tt + 1, :] = jnp.sum(w, axis=0, keepdims=True)


def _attn_sample(page_table, q3, kt_new, v_new, cache_kt, cache_v2, l1, k1, l2, k2, pps, hd, lam_init):
    nb, tn, kd = q3.shape
    nha = kd // LANES
    n_pages = page_table.shape[1]
    page = cache_kt.shape[2]
    nnew = kt_new.shape[2]
    nrow = tn * 2 * nha
    assert page == LANES and nnew == LANES and nb * tn == nnew
    pt_flat = page_table.reshape(-1)

    def k_spec(b_off):
        return pl.BlockSpec((1, kd, page),
                            lambda b, j, pt: (pt[b * n_pages + j * pps + b_off], 0, 0))

    def v_spec(b_off):
        return pl.BlockSpec((1, page * nha, LANES),
                            lambda b, j, pt: (pt[b * n_pages + j * pps + b_off], 0, 0))

    lspec = pl.BlockSpec((1, hd), lambda b, j, pt: (0, 0))
    grid_spec = pltpu.PrefetchScalarGridSpec(
        num_scalar_prefetch=1,
        grid=(nb, n_pages // pps),
        in_specs=[
            pl.BlockSpec((1, tn, kd), lambda b, j, pt: (b, 0, 0)),
            pl.BlockSpec((1, kd, nnew), lambda b, j, pt: (0, 0, 0)),
            pl.BlockSpec((1, nnew, kd), lambda b, j, pt: (0, 0, 0)),
            lspec, lspec, lspec, lspec]
        + [k_spec(b) for b in range(pps)] + [v_spec(b) for b in range(pps)],
        out_specs=pl.BlockSpec((1, tn, kd), lambda b, j, pt: (b, 0, 0)),
        scratch_shapes=[
            pltpu.VMEM((nrow, kd), BF16),
            pltpu.VMEM((nrow, LANES), F32),
            pltpu.VMEM((nrow, LANES), F32),
            pltpu.VMEM((nrow, kd), F32),
        ],
    )
    return pl.pallas_call(
        functools.partial(_decode_body, pps=pps, tn=tn, nha=nha, hd=hd, lam_init=lam_init),
        grid_spec=grid_spec,
        out_shape=SDS((nb, tn, kd), F32),
        compiler_params=_cparams(("parallel", "arbitrary")),
        name="attn_sample",
    )(pt_flat, q3, kt_new, v_new, l1, k1, l2, k2, *([cache_kt] * pps), *([cache_v2] * pps))


def _attn_out_body(o_ref, x_ref, sw_ref, w_ref, y_ref, *, nha, post_scale):
    o = o_ref[...]
    parts = []
    for h in range(nha):
        blk = o[:, h * LANES:(h + 1) * LANES]
        ms = jnp.mean(blk * blk, axis=-1, keepdims=True)
        parts.append(blk * lax.rsqrt(ms + EPS))
    on = jnp.concatenate(parts, axis=1) * sw_ref[...] * post_scale
    y_ref[...] = x_ref[...] + _dot(on.astype(BF16), w_ref[0].astype(BF16))


def _attn_out(o2, x2, sw_t, w_o, layer, tm, nha, post_scale):
    m, d = x2.shape
    vd = o2.shape[1]
    return pl.pallas_call(
        functools.partial(_attn_out_body, nha=nha, post_scale=post_scale),
        grid=(m // tm,),
        in_specs=[
            pl.BlockSpec((tm, vd), lambda i: (i, 0)),
            pl.BlockSpec((tm, d), lambda i: (i, 0)),
            pl.BlockSpec((1, vd), lambda i: (0, 0)),
            pl.BlockSpec((1, vd, d), lambda i: (layer, 0, 0)),
        ],
        out_specs=pl.BlockSpec((tm, d), lambda i: (i, 0)),
        out_shape=SDS((m, d), F32),
        compiler_params=_cparams(("parallel",)),
        name="attn_out",
    )(o2, x2, sw_t, w_o)


def kernel(x_prompt, x_sample, state_ssm, state_conv, cache_k, cache_v, page_table, norm_a, w_in_a, conv_w, conv_b, dt_bias, a_log, d_skip, gnorm_w, w_out_a, kv_norm, w_kv, k_norm, norm_b, w_q, q_norm, lam_q1, lam_k1, lam_q2, lam_k2, subln, w_o, norm_m, w_up, w_down):
    n_a = norm_a.shape[0]
    n_b = norm_b.shape[0]
    d = x_prompt.shape[-1]
    nh, hd, ds = state_ssm.shape[2], state_ssm.shape[3], state_ssm.shape[4]
    di = nh * hd
    cdim = state_conv.shape[-1]
    ng = (cdim - di) // (2 * ds)
    nha, ahd = cache_k.shape[2], cache_k.shape[4]
    kd = nha * 2 * ahd
    page = cache_k.shape[1]
    assert 2 * ahd == LANES and cache_v.shape[3] == LANES and nh <= LANES

    emat = (jnp.arange(LANES)[:, None] == (jnp.arange(di)[None, :] // hd)).astype(BF16)
    emat = jnp.concatenate([emat, emat], axis=0)
    gidx = jnp.arange(256) // ahd
    gmat = jnp.where(gidx[:, None] == gidx[None, :], 1.0 / ahd, 0.0).astype(BF16)

    def row(v):
        return v.reshape(1, -1).astype(F32)

    def pad_lanes(v):
        return jnp.pad(v.astype(F32), (0, LANES - v.shape[0])).reshape(1, LANES)

    n_in = di + cdim + nh
    tn_in = -(-n_in // (2 * LANES)) * LANES
    w_in_b = jnp.pad(w_in_a, ((0, 0), (0, 0), (0, 2 * tn_in - n_in))).astype(BF16)
    wk_t_b = w_kv[:, :kd].T.astype(BF16)
    wv_b = w_kv[:, kd:].astype(BF16)
    wq_t_b = jnp.swapaxes(w_q, 1, 2).astype(BF16)
    scale = ahd ** -0.5

    def trunk(x3, conv0, ssm0, attend, seq_major):
        nb, t, _ = x3.shape
        m = nb * t
        tm = min(512, m)
        x2 = x3.reshape(m, d)
        ab, at = (nb, t) if seq_major else (1, m)
        tma = min(512, at)
        convs, ssms = [], []
        k_new = v_new = kv_ctx = None
        for l in range(n_a + n_b):
            if l < n_a:
                zxbc = _inproj(x2, row(norm_a[l]), w_in_b, l, tm, tn_in)
                y3, hn, cn = _ssd(
                    zxbc.reshape(nb, t, 2 * tn_in), conv0, ssm0.reshape(n_a, nb, di, ds), l,
                    conv_w[l], row(conv_b[l]), pad_lanes(dt_bias[l]), pad_lanes(a_log[l]),
                    row(jnp.repeat(d_skip[l], hd)), emat, nh=nh, hd=hd, ng=ng, ds=ds)
                convs.append(cn)
                ssms.append(hn.reshape(nb, nh, hd, ds))
                tmp = min(256, m)
                if t % tmp:
                    y3 = jnp.transpose(y3, (0, 2, 1, 3)).reshape(m, di)
                x2 = _post(y3, zxbc, x2, row(gnorm_w[l]), w_out_a, l, tmp, ng)
            else:
                jb = l - n_a
                if l == n_a:
                    kv_ctx = _kv(x2.reshape(ab, at, d), row(kv_norm), wk_t_b, wv_b,
                                 jnp.tile(k_norm, 2 * nha).reshape(kd, 1).astype(F32), tma, nha, ahd, seq_major)
                    k_new = jnp.transpose(kv_ctx[0].reshape(ab, nha, 2, ahd, at), (0, 4, 1, 2, 3))
                    k_new = k_new.reshape(nb, t, nha, 2, ahd)
                    v_new = kv_ctx[1].reshape(nb, t, nha, LANES)
                lam_init = 0.8 - 0.6 * math.exp(-0.3 * l)
                lams = (row(lam_q1[jb]), row(lam_k1[jb]), row(lam_q2[jb]), row(lam_k2[jb]))
                qn_t = jnp.tile(q_norm[jb], 2 * nha).astype(F32)
                if seq_major:
                    qout = _qproj_t(x2.reshape(ab, at, d), row(norm_b[jb]), wq_t_b[jb],
                                    qn_t.reshape(kd, 1), tma, nha, ahd, scale * LOG2E)
                else:
                    qout = _qproj(x2.reshape(ab, at, d), row(norm_b[jb]), w_q, jb,
                                  qn_t.reshape(1, kd), gmat, tma, nha, scale)
                o3 = attend(qout, kv_ctx, lams, lam_init, nb, t)
                x2 = _attn_out(o3.reshape(m, kd), x2, row(jnp.tile(subln[jb], nha)), w_o, jb,
                               tm, nha, 1.0 - lam_init)
            x2 = _mlp(x2, row(norm_m[l]), w_up, w_down, l, min(1024, m), 1024)
        return x2.reshape(nb, t, d), jnp.stack(ssms), jnp.stack(convs), k_new, v_new

    def attend_prompt(qh, kv_ctx, lams, lam_init, nb, t):
        return _attn_prompt(qh, kv_ctx[2], kv_ctx[3], *lams, 512, 512, ahd, lam_init)

    n_pool = cache_k.shape[0]
    cache_kt = jnp.transpose(cache_k, (0, 2, 3, 4, 1)).reshape(n_pool, kd, page)
    cache_v2 = cache_v.reshape(n_pool, page * nha, LANES)

    def attend_sample(q3, kv_ctx, lams, lam_init, nb, t):
        return _attn_sample(page_table, q3.reshape(nb, t, kd), kv_ctx[0], kv_ctx[1],
                            cache_kt, cache_v2, *lams, 16, ahd, lam_init)

    bp = x_prompt.shape[0]
    conv0_p = jnp.zeros((n_a, bp, CONV_W - 1, cdim), F32)
    ssm0_p = jnp.zeros((n_a, bp, nh, hd, ds), F32)
    y_p, ssm_p, conv_p, k_p, v_p = trunk(x_prompt, conv0_p, ssm0_p, attend_prompt, True)
    y_s, ssm_s, conv_s, k_s, v_s = trunk(x_sample, state_conv, state_ssm, attend_sample, False)
    return (y_p, y_s, ssm_p, conv_p, k_p, v_p, ssm_s, conv_s, k_s, v_s)
```

```python
import functools
import math

import jax
import jax.numpy as jnp
from jax import lax
from jax.experimental import pallas as pl
from jax.experimental.pallas import tpu as pltpu

F32 = jnp.float32
BF16 = jnp.bfloat16
EPS = 1e-6
LANES = 128
CHUNK = 128
CONV_W = 4
VMEM_LIMIT = 48 * 1024 * 1024
MLP_VMEM_LIMIT = 56 * 1024 * 1024
NEG = -1e30
LOG2E = math.log2(math.e)
SDS = jax.ShapeDtypeStruct


def _cparams(sem):
    return pltpu.CompilerParams(dimension_semantics=sem, vmem_limit_bytes=VMEM_LIMIT)


def _rms(x, g):
    ms = jnp.mean(x * x, axis=-1, keepdims=True)
    return x * lax.rsqrt(ms + EPS) * g


def _silu(x):
    return x * (1.0 / (1.0 + jnp.exp(-x)))


def _softplus(x):
    e = jnp.exp(-jnp.abs(x))
    u = 1.0 + e
    log1p_e = jnp.where(u == 1.0, e, jnp.log(u) * (e / (u - 1.0)))
    return jnp.maximum(x, 0.0) + log1p_e


def _dot(a, b):
    return jnp.dot(a, b, preferred_element_type=F32)


def _dot_nt(a, b):
    return lax.dot_general(a, b, (((1,), (1,)), ((), ())), preferred_element_type=F32)


def _split_bf16(x):
    hi = x.astype(BF16)
    lo = (x - hi.astype(F32)).astype(BF16)
    return hi, lo


def _group_norm64(x, gmat):
    parts = []
    for c in range(x.shape[1] // 256):
        blk = x[:, c * 256:(c + 1) * 256]
        ms = _dot((blk * blk).astype(BF16), gmat)
        parts.append(blk * lax.rsqrt(ms + EPS))
    return jnp.concatenate(parts, axis=1)


def _inproj_body(x_ref, g_ref, w_ref, o_ref):
    u = _rms(x_ref[...], g_ref[...]).astype(BF16)
    o_ref[...] = _dot(u, w_ref[0])


def _inproj(x2, g, w_all, layer, tm, tn):
    m, d = x2.shape
    n = w_all.shape[2]
    return pl.pallas_call(
        _inproj_body,
        grid=(n // tn, m // tm),
        in_specs=[
            pl.BlockSpec((tm, d), lambda j, i: (i, 0)),
            pl.BlockSpec((1, d), lambda j, i: (0, 0)),
            pl.BlockSpec((1, d, tn), lambda j, i: (layer, 0, j)),
        ],
        out_specs=pl.BlockSpec((tm, tn), lambda j, i: (i, j)),
        out_shape=SDS((m, n), F32),
        compiler_params=_cparams(("parallel", "parallel")),
        name="inproj",
    )(x2, g, w_all)


def _ssd_body(xa_ref, xb_ref, dt_ref, conv0_ref, h0_ref, cw_ref, cb_ref, dtb_ref, alog_ref,
              dsk_ref, e_ref, *rest, ts, nh, hd, ng, ds, nprev):
    prev_refs, (y_ref, hn_ref, convn_ref, xpad, dtpad, hst, wsp) = rest[:len(rest) - 7], rest[-7:]
    q = CHUNK
    nph = q // 8
    di = nh * hd
    gn = ng * ds
    hpg = nh // ng
    gw = hpg * hd
    nxs = di // LANES
    ncs = xpad.shape[0]
    pad = CONV_W - 1
    s = pl.program_id(1)

    @pl.when(s == 0)
    def _():
        xpad[...] = jnp.zeros_like(xpad)
        dtpad[...] = jnp.zeros_like(dtpad)
        for c in range(ncs):
            xpad[c, 8 - pad:8, :] = conv0_ref[0, 0, :, c * LANES:(c + 1) * LANES]
        hst[...] = h0_ref[0, 0]

    for c in range(ncs):
        src = xa_ref if c < nxs else xb_ref
        cc0 = (c % nxs) * LANES
        xpad[c, 8:8 + ts, :] = src[0, :, cc0:cc0 + LANES]
        convn_ref[nprev, 0, :, c * LANES:(c + 1) * LANES] = xpad[c, 8 + ts - pad:8 + ts, :]
    dtpad[0:ts, :] = dt_ref[0]

    def phase_rows(ref2d, r0):
        return jnp.concatenate([ref2d[pl.ds(r0 + ph, nph, stride=8), :] for ph in range(8)], axis=0)

    def conv_silu(c0, width):
        slabs = []
        for c in range(c0 // LANES, (c0 + width) // LANES):
            acc = cb_ref[:, c * LANES:(c + 1) * LANES]
            for k in range(CONV_W):
                acc = acc + cw_ref[k:k + 1, c * LANES:(c + 1) * LANES] * phase_rows(xpad.at[c], 8 - pad + k)
            slabs.append(_silu(acc))
        return slabs[0] if len(slabs) == 1 else jnp.concatenate(slabs, axis=1)

    def real_time(pos):
        return ((pos & (nph - 1)) << 3) | (pos >> (nph.bit_length() - 1))

    def front(g):
        xg = conv_silu(g * gw, gw)
        bb = conv_silu(di + g * ds, ds).astype(BF16)
        cc = conv_silu(di + gn + g * ds, ds).astype(BF16)
        cbm = _dot_nt(cc, bb)
        hg = hst[g * gw:(g + 1) * gw, :]
        yoff = _dot_nt(cc, hg.astype(BF16))
        e_g = e_ref[:, g * gw:(g + 1) * gw]
        w1g = _dot(wsp[0], e_g)
        w2g = _dot(wsp[1], e_g)
        return xg, bb, cbm, hg, yoff, w1g, w2g

    dt = _softplus(phase_rows(dtpad, 0) + dtb_ref[...])
    if ts < q:
        dt = jnp.where(real_time(lax.broadcasted_iota(jnp.int32, (q, LANES), 0)) < ts, dt, 0.0)
    a = dt * (-jnp.exp(alog_ref[...]))
    causal = (real_time(lax.broadcasted_iota(jnp.int32, (q, q), 0))
              >= real_time(lax.broadcasted_iota(jnp.int32, (q, q), 1)))
    tril = jnp.where(causal, 1.0, 0.0).astype(BF16)

    a_hi, a_lo = _split_bf16(a)
    acs = _dot(jnp.concatenate([tril, tril], axis=1),
               jnp.concatenate([a_hi, a_lo], axis=0))
    acs2 = acs * LOG2E
    acs2_t = acs2.T
    dt_t = dt.T
    acs_last = acs[q - 1:q, :]
    wsp[0] = jnp.concatenate(_split_bf16(dt * jnp.exp(acs_last - acs)), axis=1)
    wsp[1] = jnp.concatenate(_split_bf16(jnp.exp(acs)), axis=1)
    cd = jnp.exp(acs_last)
    lane_g = lax.broadcasted_iota(jnp.int32, (1, gw), 1)

    nxt = front(0)
    for g in range(ng):
        xg, bb, cbm, hg, yoff, w1g, w2g = nxt
        if g + 1 < ng:
            nxt = front(g + 1)
        xgb = xg.astype(BF16)
        yd = jnp.zeros((q, gw), F32)
        for r in range(hpg):
            h = g * hpg + r
            seg = acs2[:, h:h + 1] - acs2_t[h:h + 1, :]
            dec = jnp.exp2(jnp.where(causal, seg, NEG))
            mh = (cbm * dec * dt_t[h:h + 1, :]).astype(BF16)
            yd = jnp.where((lane_g >= r * hd) & (lane_g < (r + 1) * hd), _dot(mh, xgb), yd)
        yg = yd + yoff * w2g + dsk_ref[:, g * gw:(g + 1) * gw] * xg
        for jj in range(gw // LANES):
            slab = yg[:, jj * LANES:(jj + 1) * LANES]
            cy = g * (gw // LANES) + jj
            if ts == q:
                for ph in range(8):
                    y_ref[0, cy, pl.ds(ph, nph, stride=8), :] = slab[ph * nph:(ph + 1) * nph]
            else:
                for tt in range(ts):
                    pos = (tt % 8) * nph + tt // 8
                    y_ref[0, cy, tt:tt + 1, :] = slab[pos:pos + 1]
        st = _dot((xg * w1g).T.astype(BF16), bb)
        cdt = jnp.concatenate(
            [jnp.broadcast_to(cd[:, g * hpg + r:g * hpg + r + 1], (hd, ds)) for r in range(hpg)], axis=0)
        hst[g * gw:(g + 1) * gw, :] = hg * cdt + st

    for c in range(ncs):
        xpad[c, 8 - pad:8, :] = xpad[c, 8 + ts - pad:8 + ts, :]

    @pl.when(s == pl.num_programs(1) - 1)
    def _():
        hn_ref[nprev, 0] = hst[...]
        if nprev:
            prev_h, prev_c = prev_refs
            hn_ref[0:nprev] = prev_h[...]
            convn_ref[0:nprev] = prev_c[...]


def _ssd(zxbc3, conv0, h0, layer, cw, cb, dtb, alog, dsk, emat, prev, *, nh, hd, ng, ds):
    nb, length, _ = zxbc3.shape
    di = nh * hd
    cdim = di + 2 * ng * ds
    ts = min(CHUNK, length)
    ns = length // ts
    nprev = prev[0].shape[0] if prev else 0
    body = functools.partial(_ssd_body, ts=ts, nh=nh, hd=hd, ng=ng, ds=ds, nprev=nprev)
    assert cdim == 2 * di
    dt_blk = (di + cdim) // LANES
    prev_specs = [
        pl.BlockSpec((nprev, 1, di, ds), lambda b, s: (0, b, 0, 0)),
        pl.BlockSpec((nprev, 1, CONV_W - 1, cdim), lambda b, s: (0, b, 0, 0)),
    ] if prev else []
    return pl.pallas_call(
        body,
        grid=(nb, ns),
        in_specs=[
            pl.BlockSpec((1, ts, di), lambda b, s: (b, s, 1)),
            pl.BlockSpec((1, ts, di), lambda b, s: (b, s, 2)),
            pl.BlockSpec((1, ts, LANES), lambda b, s: (b, s, dt_blk)),
            pl.BlockSpec((1, 1, CONV_W - 1, cdim), lambda b, s: (layer, b, 0, 0)),
            pl.BlockSpec((1, 1, di, ds), lambda b, s: (layer, b, 0, 0)),
            pl.BlockSpec((CONV_W, cdim), lambda b, s: (0, 0)),
            pl.BlockSpec((1, cdim), lambda b, s: (0, 0)),
            pl.BlockSpec((1, LANES), lambda b, s: (0, 0)),
            pl.BlockSpec((1, LANES), lambda b, s: (0, 0)),
            pl.BlockSpec((1, di), lambda b, s: (0, 0)),
            pl.BlockSpec((2 * LANES, di), lambda b, s: (0, 0)),
        ] + prev_specs,
        out_specs=[
            pl.BlockSpec((1, di // LANES, ts, LANES), lambda b, s: (b, 0, s, 0)),
            pl.BlockSpec((nprev + 1, 1, di, ds), lambda b, s: (0, b, 0, 0)),
            pl.BlockSpec((nprev + 1, 1, CONV_W - 1, cdim), lambda b, s: (0, b, 0, 0)),
        ],
        out_shape=[SDS((nb, di // LANES, length, LANES), F32), SDS((nprev + 1, nb, di, ds), F32),
                   SDS((nprev + 1, nb, CONV_W - 1, cdim), F32)],
        scratch_shapes=[
            pltpu.VMEM((cdim // LANES, 8 + CHUNK, LANES), F32),
            pltpu.VMEM((CHUNK, LANES), F32),
            pltpu.VMEM((di, ds), F32),
            pltpu.VMEM((2, CHUNK, 2 * LANES), BF16),
        ],
        compiler_params=_cparams(("parallel", "arbitrary")),
        name="ssd",
    )(zxbc3, zxbc3, zxbc3, conv0, h0, cw, cb, dtb, alog, dsk, emat, *prev)


def _post_body(y_ref, z_ref, x_ref, gw_ref, w_ref, o_ref, *, ng):
    if len(y_ref.shape) == 4:
        y = jnp.concatenate([y_ref[0, c] for c in range(y_ref.shape[1])], axis=1)
    else:
        y = y_ref[...]
    yz = y * _silu(z_ref[...])
    width = yz.shape[1] // ng
    parts = []
    for g in range(ng):
        blk = yz[:, g * width:(g + 1) * width]
        ms = jnp.mean(blk * blk, axis=-1, keepdims=True)
        parts.append(blk * lax.rsqrt(ms + EPS))
    yn = jnp.concatenate(parts, axis=1) * gw_ref[...]
    o_ref[...] = x_ref[...] + _dot(yn.astype(BF16), w_ref[0].astype(BF16))


def _post(y, zxbc2, x2, gw, w_out, layer, tm, ng):
    m, d = x2.shape
    if y.ndim == 4:
        nslab, length = y.shape[1], y.shape[2]
        di = nslab * LANES
        per_seq = length // tm
        y_spec = pl.BlockSpec((1, nslab, tm, LANES), lambda i: (i // per_seq, 0, i % per_seq, 0))
    else:
        di = y.shape[1]
        y_spec = pl.BlockSpec((tm, di), lambda i: (i, 0))
    return pl.pallas_call(
        functools.partial(_post_body, ng=ng),
        grid=(m // tm,),
        in_specs=[
            y_spec,
            pl.BlockSpec((tm, di), lambda i: (i, 0)),
            pl.BlockSpec((tm, d), lambda i: (i, 0)),
            pl.BlockSpec((1, di), lambda i: (0, 0)),
            pl.BlockSpec((1, di, d), lambda i: (layer, 0, 0)),
        ],
        out_specs=pl.BlockSpec((tm, d), lambda i: (i, 0)),
        out_shape=SDS((m, d), F32),
        compiler_params=_cparams(("parallel",)),
        name="post_mamba",
    )(y, zxbc2, x2, gw, w_out)


def _mlp_body(x_ref, g_ref, wu_ref, wd_ref, o_ref, u_scr):
    c = pl.program_id(1)

    @pl.when(c == 0)
    def _():
        u_scr[...] = _rms(x_ref[...], g_ref[...]).astype(BF16)

    u = u_scr[...]
    half = wu_ref.shape[2] // 2
    hs = [jnp.maximum(_dot(u, wu_ref[0, :, k * half:(k + 1) * half].astype(BF16)), 0.0) for k in range(2)]
    part = _dot((hs[0] * hs[0]).astype(BF16), wd_ref[0, 0:half, :].astype(BF16))
    part = part + _dot((hs[1] * hs[1]).astype(BF16), wd_ref[0, half:2 * half, :].astype(BF16))

    @pl.when(c == 0)
    def _():
        o_ref[...] = x_ref[...] + part

    @pl.when(c != 0)
    def _():
        o_ref[...] += part


def _mlp(x2, g, w_up, w_down, layer, tm, tf):
    m, d = x2.shape
    ff = w_up.shape[2]
    return pl.pallas_call(
        _mlp_body,
        grid=(m // tm, ff // tf),
        in_specs=[
            pl.BlockSpec((tm, d), lambda i, c: (i, 0)),
            pl.BlockSpec((1, d), lambda i, c: (0, 0)),
            pl.BlockSpec((1, d, tf), lambda i, c: (layer, 0, c)),
            pl.BlockSpec((1, tf, d), lambda i, c: (layer, c, 0)),
        ],
        out_specs=pl.BlockSpec((tm, d), lambda i, c: (i, 0)),
        out_shape=SDS((m, d), F32),
        scratch_shapes=[pltpu.VMEM((tm, d), BF16)],
        compiler_params=pltpu.CompilerParams(dimension_semantics=("parallel", "arbitrary"),
                                             vmem_limit_bytes=MLP_VMEM_LIMIT),
        name="mlp",
    )(x2, g, w_up, w_down)


def _row_group_norm(xt, width):
    n, tm = xt.shape
    x3 = xt.reshape(n // width, width, tm)
    ms = jnp.mean(x3 * x3, axis=1, keepdims=True)
    return (x3 * lax.rsqrt(ms + EPS)).reshape(n, tm)


def _kv_body(x_ref, g_ref, wkt_ref, wv_ref, kn_ref, kt_ref, v_ref, *head_refs, nha, ahd):
    u = _rms(x_ref[0], g_ref[...]).astype(BF16)
    kt = _row_group_norm(_dot_nt(wkt_ref[...], u), ahd) * kn_ref[...]
    v = _dot(u, wv_ref[...])
    kt_ref[0] = kt
    v_ref[0] = v
    if head_refs:
        kh_ref, vht_ref = head_refs
        k = kt.T
        for h in range(nha):
            kh_ref[0, h] = k[:, h * LANES:(h + 1) * LANES].astype(BF16)
            vht_ref[0, h] = v[:, h * LANES:(h + 1) * LANES].T.astype(BF16)


def _kv(x3, g, wk_t, wv, kn_col, tm, nha, ahd, head_major):
    nb, t, d = x3.shape
    kd = nha * LANES
    out_specs = [
        pl.BlockSpec((1, kd, tm), lambda b, i: (b, 0, i)),
        pl.BlockSpec((1, tm, kd), lambda b, i: (b, i, 0)),
    ]
    out_shape = [SDS((nb, kd, t), F32), SDS((nb, t, kd), F32)]
    if head_major:
        out_specs += [
            pl.BlockSpec((1, nha, tm, LANES), lambda b, i: (b, 0, i, 0)),
            pl.BlockSpec((1, nha, LANES, tm), lambda b, i: (b, 0, 0, i)),
        ]
        out_shape += [SDS((nb, nha, t, LANES), BF16), SDS((nb, nha, LANES, t), BF16)]
    return pl.pallas_call(
        functools.partial(_kv_body, nha=nha, ahd=ahd),
        grid=(nb, t // tm),
        in_specs=[
            pl.BlockSpec((1, tm, d), lambda b, i: (b, i, 0)),
            pl.BlockSpec((1, d), lambda b, i: (0, 0)),
            pl.BlockSpec((kd, d), lambda b, i: (0, 0)),
            pl.BlockSpec((d, kd), lambda b, i: (0, 0)),
            pl.BlockSpec((kd, 1), lambda b, i: (0, 0)),
        ],
        out_specs=out_specs,
        out_shape=out_shape,
        compiler_params=_cparams(("parallel", "parallel")),
        name="kv_proj",
    )(x3, g, wk_t, wv, kn_col)


def _q_body(x_ref, g_ref, w_ref, qn_ref, gm_ref, q_ref, *, scale):
    u = _rms(x_ref[0], g_ref[...]).astype(BF16)
    qn = _group_norm64(_dot(u, w_ref[0].astype(BF16)), gm_ref[...]) * qn_ref[...] * scale
    q_ref[0] = qn.astype(BF16)


def _qproj(x3, g, w_q, layer, qn_t, gmat, tm, nha, scale):
    nb, t, d = x3.shape
    kd = nha * LANES
    return pl.pallas_call(
        functools.partial(_q_body, scale=scale),
        grid=(nb, t // tm),
        in_specs=[
            pl.BlockSpec((1, tm, d), lambda b, i: (b, i, 0)),
            pl.BlockSpec((1, d), lambda b, i: (0, 0)),
            pl.BlockSpec((1, d, kd), lambda b, i: (layer, 0, 0)),
            pl.BlockSpec((1, kd), lambda b, i: (0, 0)),
            pl.BlockSpec((256, 256), lambda b, i: (0, 0)),
        ],
        out_specs=pl.BlockSpec((1, tm, kd), lambda b, i: (b, i, 0)),
        out_shape=SDS((nb, t, kd), BF16),
        compiler_params=_cparams(("parallel", "parallel")),
        name="q_proj",
    )(x3, g, w_q, qn_t, gmat)


def _lam(l1, k1, l2, k2, lam_init):
    return (jnp.exp(jnp.sum(l1 * k1, axis=-1, keepdims=True))
            - jnp.exp(jnp.sum(l2 * k2, axis=-1, keepdims=True)) + lam_init)


def _attn_body(it_ref, jt_ref, x_ref, g_ref, wqt_ref, qn_ref, k_ref, vt_ref,
               l1_ref, k1_ref, l2_ref, k2_ref, sw_ref, wo_ref, y_ref,
               qt_ref, m_scr, l_scr, acc_scr, st_scr, mx_scr, *, tq, tk, nha, ahd, qscale, lam_init):
    t = pl.program_id(1)
    i = it_ref[t]
    j = jt_ref[t]

    @pl.when(j == 0)
    def _():
        m_scr[...] = jnp.full_like(m_scr, NEG)
        l_scr[...] = jnp.zeros_like(l_scr)
        acc_scr[...] = jnp.zeros_like(acc_scr)
        u = _rms(x_ref[0], g_ref[...]).astype(BF16)
        qt = _row_group_norm(_dot_nt(wqt_ref[0], u), ahd) * (qn_ref[...] * qscale)
        rows = lax.broadcasted_iota(jnp.int32, (LANES, tq), 0)
        for h in range(nha):
            blk = qt[h * LANES:(h + 1) * LANES, :]
            qt_ref[h, 0] = jnp.where(rows < ahd, blk, 0.0).astype(BF16)
            qt_ref[h, 1] = jnp.where(rows >= ahd, blk, 0.0).astype(BF16)

    def scores(h, c, slot, bias):
        st = _dot(k_ref[0, h], qt_ref[h, c])
        if bias is not None:
            st = st + bias
        st_scr[slot] = st
        mx_scr[slot] = jnp.max(st, axis=0, keepdims=True)

    def accumulate(h, c, slot):
        m_prev = m_scr[h, c]
        m_new = jnp.maximum(m_prev, mx_scr[slot])
        alpha = jnp.exp2(m_prev - m_new)
        p = jnp.exp2(st_scr[slot] - m_new)
        l_scr[h, c] = alpha * l_scr[h, c] + jnp.sum(p, axis=0, keepdims=True)
        acc_scr[h, c] = alpha * acc_scr[h, c] + _dot(vt_ref[0, h], p.astype(BF16))
        m_scr[h, c] = m_new

    def run_heads(bias):
        scores(0, 0, 0, bias)

        def head(h, carry):
            scores(h, 1, 1, bias)
            accumulate(h, 0, 0)
            scores(jnp.minimum(h + 1, nha - 1), 0, 0, bias)
            accumulate(h, 1, 1)
            return carry

        lax.fori_loop(0, nha, head, 0, unroll=4)

    needs_mask = (j + 1) * tk - 1 > i * tq

    @pl.when(needs_mask)
    def _():
        kpos = j * tk + lax.broadcasted_iota(jnp.int32, (tk, tq), 0)
        qpos = i * tq + lax.broadcasted_iota(jnp.int32, (tk, tq), 1)
        run_heads(jnp.where(kpos <= qpos, 0.0, NEG))

    @pl.when(jnp.logical_not(needs_mask))
    def _():
        run_heads(None)

    @pl.when((j + 1) * tk >= (i + 1) * tq)
    def _():
        lam = _lam(l1_ref[...], k1_ref[...], l2_ref[...], k2_ref[...], lam_init)
        parts = []
        for h in range(nha):
            ot = (acc_scr[h, 0] * (1.0 / l_scr[h, 0])
                  - (lam * (1.0 / l_scr[h, 1])) * acc_scr[h, 1])
            ms = jnp.mean(ot * ot, axis=0, keepdims=True)
            parts.append((ot * lax.rsqrt(ms + EPS)).T)
        on = jnp.concatenate(parts, axis=1) * (sw_ref[...] * (1.0 - lam_init))
        y_ref[0] = x_ref[0] + _dot(on.astype(BF16), wo_ref[0].astype(BF16))


def _attn_prompt(x3, g, wq_t, qn_col, kh, vht, l1, k1, l2, k2, sw_t, w_o, layer, tq, tk, ahd, qscale,
                 lam_init):
    nb, t, d = x3.shape
    nha = kh.shape[1]
    kd = nha * LANES
    hd = ahd
    its, jts = [], []
    for i in range(t // tq):
        for j in range(((i + 1) * tq + tk - 1) // tk):
            its.append(i)
            jts.append(j)
    it = jnp.asarray(its, jnp.int32)
    jt = jnp.asarray(jts, jnp.int32)
    lspec = pl.BlockSpec((1, hd), lambda b, s, it, jt: (0, 0))
    grid_spec = pltpu.PrefetchScalarGridSpec(
        num_scalar_prefetch=2,
        grid=(nb, len(its)),
        in_specs=[
            pl.BlockSpec((1, tq, d), lambda b, s, it, jt: (b, it[s], 0)),
            pl.BlockSpec((1, d), lambda b, s, it, jt: (0, 0)),
            pl.BlockSpec((1, kd, d), lambda b, s, it, jt: (layer, 0, 0)),
            pl.BlockSpec((kd, 1), lambda b, s, it, jt: (0, 0)),
            pl.BlockSpec((1, nha, tk, LANES), lambda b, s, it, jt: (b, 0, jt[s], 0)),
            pl.BlockSpec((1, nha, LANES, tk), lambda b, s, it, jt: (b, 0, 0, jt[s])),
            lspec, lspec, lspec, lspec,
            pl.BlockSpec((1, kd), lambda b, s, it, jt: (0, 0)),
            pl.BlockSpec((1, kd, d), lambda b, s, it, jt: (layer, 0, 0)),
        ],
        out_specs=pl.BlockSpec((1, tq, d), lambda b, s, it, jt: (b, it[s], 0)),
        scratch_shapes=[
            pltpu.VMEM((nha, 2, LANES, tq), BF16),
            pltpu.VMEM((nha, 2, 1, tq), F32),
            pltpu.VMEM((nha, 2, 1, tq), F32),
            pltpu.VMEM((nha, 2, LANES, tq), F32),
            pltpu.VMEM((2, tk, tq), F32),
            pltpu.VMEM((2, 1, tq), F32),
        ],
    )
    return pl.pallas_call(
        functools.partial(_attn_body, tq=tq, tk=tk, nha=nha, ahd=ahd, qscale=qscale, lam_init=lam_init),
        grid_spec=grid_spec,
        out_shape=SDS((nb, t, d), F32),
        compiler_params=_cparams(("parallel", "arbitrary")),
        name="attn_prompt",
    )(it, jt, x3, g, wq_t, qn_col, kh, vht, l1, k1, l2, k2, sw_t, w_o)


def _decode_body(pt_ref, q_ref, kn_ref, vn_ref, l1_ref, k1_ref, l2_ref, k2_ref, *rest,
                 pps, tn, nha, hd, lam_init):
    k_refs = rest[:pps]
    v_refs = rest[pps:2 * pps]
    o_ref = rest[2 * pps]
    qb, m_scr, l_scr, acc_scr = rest[2 * pps + 1:]
    b_idx = pl.program_id(0)
    j = pl.program_id(1)
    nmap = 2 * nha
    nrow = tn * nmap
    kd = nha * LANES
    page = k_refs[0].shape[2]

    def online(sc, vals):
        n = len(vals)
        m_prev = m_scr[...]
        blk_max = sc[:, 0:LANES]
        for b in range(1, n):
            blk_max = jnp.maximum(blk_max, sc[:, b * LANES:(b + 1) * LANES])
        m_new = jnp.maximum(m_prev, jnp.max(blk_max, axis=-1, keepdims=True))
        alpha = jnp.exp(m_prev - m_new)
        p = jnp.exp(sc - pltpu.repeat(m_new, n, axis=1))
        psum = p[:, 0:LANES]
        for b in range(1, n):
            psum = psum + p[:, b * LANES:(b + 1) * LANES]
        l_scr[...] = alpha * l_scr[...] + psum
        pv = _dot(p[:, 0:LANES].astype(BF16), vals[0])
        for b in range(1, n):
            pv = pv + _dot(p[:, b * LANES:(b + 1) * LANES].astype(BF16), vals[b])
        acc_scr[...] = pltpu.repeat(alpha, kd // LANES, axis=1) * acc_scr[...] + pv
        m_scr[...] = m_new

    def page_values(v_ref):
        return jnp.concatenate(
            [v_ref[0, pl.ds(h, page, stride=nha), :] for h in range(nha)], axis=1).astype(BF16)

    @pl.when(j == 0)
    def _():
        rid = lax.broadcasted_iota(jnp.int32, (nmap, kd), 0)
        lid = lax.broadcasted_iota(jnp.int32, (nmap, kd), 1)
        sel = (lid >= rid * hd) & (lid < (rid + 1) * hd)
        qf = q_ref[0].astype(F32)
        for tt in range(tn):
            rowq = jnp.broadcast_to(qf[tt:tt + 1, :], (nmap, kd))
            qb[tt * nmap:(tt + 1) * nmap, :] = jnp.where(sel, rowq, 0.0).astype(BF16)
        m_scr[...] = jnp.full_like(m_scr, NEG)
        l_scr[...] = jnp.zeros_like(l_scr)
        acc_scr[...] = jnp.zeros_like(acc_scr)
        nnew = kn_ref.shape[2]
        qrow = lax.broadcasted_iota(jnp.int32, (nrow, LANES), 0)
        ktok = lax.broadcasted_iota(jnp.int32, (nrow, LANES), 1) - b_idx * tn
        for blk in range(nnew // LANES):
            sc = _dot(qb[...], kn_ref[0, :, blk * LANES:(blk + 1) * LANES].astype(BF16))
            kt = ktok - blk * LANES
            sc = jnp.where((kt >= 0) & (kt * nmap <= qrow), sc, NEG)
            online(sc, [vn_ref[0, blk * LANES:(blk + 1) * LANES, :].astype(BF16)])

    scs = [_dot(qb[...], k_refs[b][0].astype(BF16)) for b in range(pps)]
    online(jnp.concatenate(scs, axis=1), [page_values(v_refs[b]) for b in range(pps)])

    @pl.when(j == pl.num_programs(1) - 1)
    def _():
        lam = _lam(l1_ref[...], k1_ref[...], l2_ref[...], k2_ref[...], lam_init)
        inv_l = 1.0 / jnp.sum(l_scr[...], axis=-1, keepdims=True)
        rid = lax.broadcasted_iota(jnp.int32, (nmap, 1), 0)
        rid2 = lax.broadcasted_iota(jnp.int32, (nmap, kd), 0)
        lid2 = lax.broadcasted_iota(jnp.int32, (nmap, kd), 1)
        own = (lid2 >= (rid2 >> 1) * LANES) & (lid2 < ((rid2 >> 1) + 1) * LANES)
        for tt in range(tn):
            il = inv_l[tt * nmap:(tt + 1) * nmap, :]
            coef = jnp.where((rid & 1) == 0, il, -lam * il)
            w = jnp.where(own, acc_scr[tt * nmap:(tt + 1) * nmap, :] * coef, 0.0)
            o_ref[0, tt:tt + 1, :] = jnp.sum(w, axis=0, keepdims=True)


def _attn_sample(page_table, q3, kt_new, v_new, cache_kt, cache_v2, l1, k1, l2, k2, pps, hd, lam_init):
    nb, tn, kd = q3.shape
    nha = kd // LANES
    n_pages = page_table.shape[1]
    page = cache_kt.shape[2]
    nnew = kt_new.shape[2]
    nrow = tn * 2 * nha
    assert page == LANES and nnew == LANES and nb * tn == nnew
    pt_flat = page_table.reshape(-1)

    def k_spec(b_off):
        return pl.BlockSpec((1, kd, page),
                            lambda b, j, pt: (pt[b * n_pages + j * pps + b_off], 0, 0))

    def v_spec(b_off):
        return pl.BlockSpec((1, page * nha, LANES),
                            lambda b, j, pt: (pt[b * n_pages + j * pps + b_off], 0, 0))

    lspec = pl.BlockSpec((1, hd), lambda b, j, pt: (0, 0))
    grid_spec = pltpu.PrefetchScalarGridSpec(
        num_scalar_prefetch=1,
        grid=(nb, n_pages // pps),
        in_specs=[
            pl.BlockSpec((1, tn, kd), lambda b, j, pt: (b, 0, 0)),
            pl.BlockSpec((1, kd, nnew), lambda b, j, pt: (0, 0, 0)),
            pl.BlockSpec((1, nnew, kd), lambda b, j, pt: (0, 0, 0)),
            lspec, lspec, lspec, lspec]
        + [k_spec(b) for b in range(pps)] + [v_spec(b) for b in range(pps)],
        out_specs=pl.BlockSpec((1, tn, kd), lambda b, j, pt: (b, 0, 0)),
        scratch_shapes=[
            pltpu.VMEM((nrow, kd), BF16),
            pltpu.VMEM((nrow, LANES), F32),
            pltpu.VMEM((nrow, LANES), F32),
            pltpu.VMEM((nrow, kd), F32),
        ],
    )
    return pl.pallas_call(
        functools.partial(_decode_body, pps=pps, tn=tn, nha=nha, hd=hd, lam_init=lam_init),
        grid_spec=grid_spec,
        out_shape=SDS((nb, tn, kd), F32),
        compiler_params=_cparams(("parallel", "arbitrary")),
        name="attn_sample",
    )(pt_flat, q3, kt_new, v_new, l1, k1, l2, k2, *([cache_kt] * pps), *([cache_v2] * pps))


def _attn_out_body(o_ref, x_ref, sw_ref, w_ref, y_ref, *, nha, post_scale):
    o = o_ref[...]
    parts = []
    for h in range(nha):
        blk = o[:, h * LANES:(h + 1) * LANES]
        ms = jnp.mean(blk * blk, axis=-1, keepdims=True)
        parts.append(blk * lax.rsqrt(ms + EPS))
    on = jnp.concatenate(parts, axis=1) * sw_ref[...] * post_scale
    y_ref[...] = x_ref[...] + _dot(on.astype(BF16), w_ref[0].astype(BF16))


def _attn_out(o2, x2, sw_t, w_o, layer, tm, nha, post_scale):
    m, d = x2.shape
    vd = o2.shape[1]
    return pl.pallas_call(
        functools.partial(_attn_out_body, nha=nha, post_scale=post_scale),
        grid=(m // tm,),
        in_specs=[
            pl.BlockSpec((tm, vd), lambda i: (i, 0)),
            pl.BlockSpec((tm, d), lambda i: (i, 0)),
            pl.BlockSpec((1, vd), lambda i: (0, 0)),
            pl.BlockSpec((1, vd, d), lambda i: (layer, 0, 0)),
        ],
        out_specs=pl.BlockSpec((tm, d), lambda i: (i, 0)),
        out_shape=SDS((m, d), F32),
        compiler_params=_cparams(("parallel",)),
        name="attn_out",
    )(o2, x2, sw_t, w_o)


def kernel(x_prompt, x_sample, state_ssm, state_conv, cache_k, cache_v, page_table, norm_a, w_in_a, conv_w, conv_b, dt_bias, a_log, d_skip, gnorm_w, w_out_a, kv_norm, w_kv, k_norm, norm_b, w_q, q_norm, lam_q1, lam_k1, lam_q2, lam_k2, subln, w_o, norm_m, w_up, w_down):
    n_a = norm_a.shape[0]
    n_b = norm_b.shape[0]
    d = x_prompt.shape[-1]
    nh, hd, ds = state_ssm.shape[2], state_ssm.shape[3], state_ssm.shape[4]
    di = nh * hd
    cdim = state_conv.shape[-1]
    ng = (cdim - di) // (2 * ds)
    nha, ahd = cache_k.shape[2], cache_k.shape[4]
    kd = nha * 2 * ahd
    page = cache_k.shape[1]
    assert 2 * ahd == LANES and cache_v.shape[3] == LANES and nh <= LANES

    emat = (jnp.arange(LANES)[:, None] == (jnp.arange(di)[None, :] // hd)).astype(BF16)
    emat = jnp.concatenate([emat, emat], axis=0)
    gidx = jnp.arange(256) // ahd
    gmat = jnp.where(gidx[:, None] == gidx[None, :], 1.0 / ahd, 0.0).astype(BF16)

    def row(v):
        return v.reshape(1, -1).astype(F32)

    def pad_lanes(v):
        return jnp.pad(v.astype(F32), (0, LANES - v.shape[0])).reshape(1, LANES)

    n_in = di + cdim + nh
    tn_in = -(-n_in // (2 * LANES)) * LANES
    w_in_b = jnp.pad(w_in_a, ((0, 0), (0, 0), (0, 2 * tn_in - n_in))).astype(BF16)
    wk_t_b = w_kv[:, :kd].T.astype(BF16)
    wv_b = w_kv[:, kd:].astype(BF16)
    wq_t_b = jnp.swapaxes(w_q, 1, 2).astype(BF16)
    scale = ahd ** -0.5

    def trunk(x3, conv0, ssm0, seq_major):
        nb, t, _ = x3.shape
        m = nb * t
        tm = min(512, m)
        x2 = x3.reshape(m, d)
        ab, at = (nb, t) if seq_major else (1, m)
        tma = min(512, at)
        states = ()
        k_new = v_new = kv_ctx = None
        for l in range(n_a + n_b):
            if l < n_a:
                zxbc = _inproj(x2, row(norm_a[l]), w_in_b, l, tm, tn_in)
                y3, hn, cn = _ssd(
                    zxbc.reshape(nb, t, 2 * tn_in), conv0, ssm0.reshape(n_a, nb, di, ds), l,
                    conv_w[l], row(conv_b[l]), pad_lanes(dt_bias[l]), pad_lanes(a_log[l]),
                    row(jnp.repeat(d_skip[l], hd)), emat, states, nh=nh, hd=hd, ng=ng, ds=ds)
                states = (hn, cn)
                tmp = min(256, m)
                if t % tmp:
                    y3 = jnp.transpose(y3, (0, 2, 1, 3)).reshape(m, di)
                x2 = _post(y3, zxbc, x2, row(gnorm_w[l]), w_out_a, l, tmp, ng)
            else:
                jb = l - n_a
                if l == n_a:
                    kv_ctx = _kv(x2.reshape(ab, at, d), row(kv_norm), wk_t_b, wv_b,
                                 jnp.tile(k_norm, 2 * nha).reshape(kd, 1).astype(F32), tma, nha, ahd, seq_major)
                    k_new = jnp.transpose(kv_ctx[0].reshape(ab, nha, 2, ahd, at), (0, 4, 1, 2, 3))
                    k_new = k_new.reshape(nb, t, nha, 2, ahd)
                    v_new = kv_ctx[1].reshape(nb, t, nha, LANES)
                lam_init = 0.8 - 0.6 * math.exp(-0.3 * l)
                lams = (row(lam_q1[jb]), row(lam_k1[jb]), row(lam_q2[jb]), row(lam_k2[jb]))
                qn_t = jnp.tile(q_norm[jb], 2 * nha).astype(F32)
                sw_t = row(jnp.tile(subln[jb], nha))
                if seq_major:
                    x2 = _attn_prompt(x2.reshape(nb, t, d), row(norm_b[jb]), wq_t_b, qn_t.reshape(kd, 1),
                                      kv_ctx[2], kv_ctx[3], *lams, sw_t, w_o, jb, 512, 512, ahd,
                                      scale * LOG2E, lam_init).reshape(m, d)
                else:
                    q3 = _qproj(x2.reshape(ab, at, d), row(norm_b[jb]), w_q, jb,
                                qn_t.reshape(1, kd), gmat, tma, nha, scale)
                    o3 = attend_sample(q3, kv_ctx, lams, lam_init, nb, t)
                    x2 = _attn_out(o3.reshape(m, kd), x2, sw_t, w_o, jb, tm, nha, 1.0 - lam_init)
            x2 = _mlp(x2, row(norm_m[l]), w_up, w_down, l, min(1024, m), 1024)
        return x2.reshape(nb, t, d), states[0].reshape(n_a, nb, nh, hd, ds), states[1], k_new, v_new

    n_pool = cache_k.shape[0]
    cache_kt = jnp.transpose(cache_k, (0, 2, 3, 4, 1)).reshape(n_pool, kd, page)
    cache_v2 = cache_v.reshape(n_pool, page * nha, LANES)

    def attend_sample(q3, kv_ctx, lams, lam_init, nb, t):
        return _attn_sample(page_table, q3.reshape(nb, t, kd), kv_ctx[0], kv_ctx[1],
                            cache_kt, cache_v2, *lams, 16, ahd, lam_init)

    bp = x_prompt.shape[0]
    conv0_p = jnp.zeros((n_a, bp, CONV_W - 1, cdim), F32)
    ssm0_p = jnp.zeros((n_a, bp, nh, hd, ds), F32)
    y_p, ssm_p, conv_p, k_p, v_p = trunk(x_prompt, conv0_p, ssm0_p, True)
    y_s, ssm_s, conv_s, k_s, v_s = trunk(x_sample, state_conv, state_ssm, False)
    return (y_p, y_s, ssm_p, conv_p, k_p, v_p, ssm_s, conv_s, k_s, v_s)
```

```python
import functools
import math

import jax
import jax.numpy as jnp
from jax import lax
from jax.experimental import pallas as pl
from jax.experimental.pallas import tpu as pltpu

F32 = jnp.float32
BF16 = jnp.bfloat16
EPS = 1e-6
LANES = 128
CHUNK = 128
CONV_W = 4
VMEM_LIMIT = 48 * 1024 * 1024
MLP_VMEM_LIMIT = 56 * 1024 * 1024
NEG = -1e30
LOG2E = math.log2(math.e)
SDS = jax.ShapeDtypeStruct


def _cparams(sem):
    return pltpu.CompilerParams(dimension_semantics=sem, vmem_limit_bytes=VMEM_LIMIT)


def _rms(x, g):
    ms = jnp.mean(x * x, axis=-1, keepdims=True)
    return x * lax.rsqrt(ms + EPS) * g


def _silu(x):
    return x * (1.0 / (1.0 + jnp.exp(-x)))


def _softplus(x):
    e = jnp.exp(-jnp.abs(x))
    u = 1.0 + e
    log1p_e = jnp.where(u == 1.0, e, jnp.log(u) * (e / (u - 1.0)))
    return jnp.maximum(x, 0.0) + log1p_e


def _dot(a, b):
    return jnp.dot(a, b, preferred_element_type=F32)


def _dot_nt(a, b):
    return lax.dot_general(a, b, (((1,), (1,)), ((), ())), preferred_element_type=F32)


def _split_bf16(x):
    hi = x.astype(BF16)
    lo = (x - hi.astype(F32)).astype(BF16)
    return hi, lo


def _group_norm64(x, gmat):
    parts = []
    for c in range(x.shape[1] // 256):
        blk = x[:, c * 256:(c + 1) * 256]
        ms = _dot((blk * blk).astype(BF16), gmat)
        parts.append(blk * lax.rsqrt(ms + EPS))
    return jnp.concatenate(parts, axis=1)


def _inproj_body(x_ref, g_ref, w_ref, o_ref):
    u = _rms(x_ref[...], g_ref[...]).astype(BF16)
    o_ref[...] = _dot(u, w_ref[0])


def _inproj(x2, g, w_all, layer, tm, tn):
    m, d = x2.shape
    n = w_all.shape[2]
    return pl.pallas_call(
        _inproj_body,
        grid=(n // tn, m // tm),
        in_specs=[
            pl.BlockSpec((tm, d), lambda j, i: (i, 0)),
            pl.BlockSpec((1, d), lambda j, i: (0, 0)),
            pl.BlockSpec((1, d, tn), lambda j, i: (layer, 0, j)),
        ],
        out_specs=pl.BlockSpec((tm, tn), lambda j, i: (i, j)),
        out_shape=SDS((m, n), F32),
        compiler_params=_cparams(("parallel", "parallel")),
        name="inproj",
    )(x2, g, w_all)


def _ssd_body(xa_ref, xb_ref, dt_ref, conv0_ref, h0_ref, cw_ref, cb_ref, dtb_ref, alog_ref,
              dsk_ref, e_ref, *rest, ts, nh, hd, ng, ds, nprev):
    prev_refs, (y_ref, hn_ref, convn_ref, xpad, dtpad, hst, wsp) = rest[:len(rest) - 7], rest[-7:]
    q = CHUNK
    nph = q // 8
    di = nh * hd
    gn = ng * ds
    hpg = nh // ng
    gw = hpg * hd
    nxs = di // LANES
    ncs = xpad.shape[0]
    pad = CONV_W - 1
    s = pl.program_id(1)

    @pl.when(s == 0)
    def _():
        xpad[...] = jnp.zeros_like(xpad)
        dtpad[...] = jnp.zeros_like(dtpad)
        for c in range(ncs):
            xpad[c, 8 - pad:8, :] = conv0_ref[0, 0, :, c * LANES:(c + 1) * LANES]
        hst[...] = h0_ref[0, 0]

    for c in range(ncs):
        src = xa_ref if c < nxs else xb_ref
        cc0 = (c % nxs) * LANES
        xpad[c, 8:8 + ts, :] = src[0, :, cc0:cc0 + LANES]
        convn_ref[nprev, 0, :, c * LANES:(c + 1) * LANES] = xpad[c, 8 + ts - pad:8 + ts, :]
    dtpad[0:ts, :] = dt_ref[0]

    def phase_rows(ref2d, r0):
        return jnp.concatenate([ref2d[pl.ds(r0 + ph, nph, stride=8), :] for ph in range(8)], axis=0)

    def conv_silu(c0, width):
        slabs = []
        for c in range(c0 // LANES, (c0 + width) // LANES):
            acc = cb_ref[:, c * LANES:(c + 1) * LANES]
            for k in range(CONV_W):
                acc = acc + cw_ref[k:k + 1, c * LANES:(c + 1) * LANES] * phase_rows(xpad.at[c], 8 - pad + k)
            slabs.append(_silu(acc))
        return slabs[0] if len(slabs) == 1 else jnp.concatenate(slabs, axis=1)

    def real_time(pos):
        return ((pos & (nph - 1)) << 3) | (pos >> (nph.bit_length() - 1))

    def front(g):
        xg = conv_silu(g * gw, gw)
        bb = conv_silu(di + g * ds, ds).astype(BF16)
        cc = conv_silu(di + gn + g * ds, ds).astype(BF16)
        cbm = _dot_nt(cc, bb)
        hg = hst[g * gw:(g + 1) * gw, :]
        yoff = _dot_nt(cc, hg.astype(BF16))
        e_g = e_ref[:, g * gw:(g + 1) * gw]
        w1g = _dot(wsp[0], e_g)
        w2g = _dot(wsp[1], e_g)
        return xg, bb, cbm, hg, yoff, w1g, w2g

    dt = _softplus(phase_rows(dtpad, 0) + dtb_ref[...])
    if ts < q:
        dt = jnp.where(real_time(lax.broadcasted_iota(jnp.int32, (q, LANES), 0)) < ts, dt, 0.0)
    a = dt * (-jnp.exp(alog_ref[...]))
    causal = (real_time(lax.broadcasted_iota(jnp.int32, (q, q), 0))
              >= real_time(lax.broadcasted_iota(jnp.int32, (q, q), 1)))
    tril = jnp.where(causal, 1.0, 0.0).astype(BF16)

    a_hi, a_lo = _split_bf16(a)
    acs = _dot(jnp.concatenate([tril, tril], axis=1),
               jnp.concatenate([a_hi, a_lo], axis=0))
    acs2 = acs * LOG2E
    acs2_t = acs2.T
    dt_t = dt.T
    acs_last = acs[q - 1:q, :]
    wsp[0] = jnp.concatenate(_split_bf16(dt * jnp.exp(acs_last - acs)), axis=1)
    wsp[1] = jnp.concatenate(_split_bf16(jnp.exp(acs)), axis=1)
    cd = jnp.exp(acs_last)
    lane_g = lax.broadcasted_iota(jnp.int32, (1, gw), 1)

    nxt = front(0)
    for g in range(ng):
        xg, bb, cbm, hg, yoff, w1g, w2g = nxt
        if g + 1 < ng:
            nxt = front(g + 1)
        xgb = xg.astype(BF16)
        yd = jnp.zeros((q, gw), F32)
        for r in range(hpg):
            h = g * hpg + r
            seg = acs2[:, h:h + 1] - acs2_t[h:h + 1, :]
            dec = jnp.exp2(jnp.where(causal, seg, NEG))
            mh = (cbm * dec * dt_t[h:h + 1, :]).astype(BF16)
            yd = jnp.where((lane_g >= r * hd) & (lane_g < (r + 1) * hd), _dot(mh, xgb), yd)
        yg = yd + yoff * w2g + dsk_ref[:, g * gw:(g + 1) * gw] * xg
        for jj in range(gw // LANES):
            slab = yg[:, jj * LANES:(jj + 1) * LANES]
            cy = g * (gw // LANES) + jj
            if ts == q:
                for ph in range(8):
                    y_ref[0, cy, pl.ds(ph, nph, stride=8), :] = slab[ph * nph:(ph + 1) * nph]
            else:
                for tt in range(ts):
                    pos = (tt % 8) * nph + tt // 8
                    y_ref[0, cy, tt:tt + 1, :] = slab[pos:pos + 1]
        st = _dot((xg * w1g).T.astype(BF16), bb)
        cdt = jnp.concatenate(
            [jnp.broadcast_to(cd[:, g * hpg + r:g * hpg + r + 1], (hd, ds)) for r in range(hpg)], axis=0)
        hst[g * gw:(g + 1) * gw, :] = hg * cdt + st

    for c in range(ncs):
        xpad[c, 8 - pad:8, :] = xpad[c, 8 + ts - pad:8 + ts, :]

    @pl.when(s == pl.num_programs(1) - 1)
    def _():
        hn_ref[nprev, 0] = hst[...]
        if nprev:
            prev_h, prev_c = prev_refs
            hn_ref[0:nprev] = prev_h[...]
            convn_ref[0:nprev] = prev_c[...]


def _ssd(zxbc3, conv0, h0, layer, cw, cb, dtb, alog, dsk, emat, prev, *, nh, hd, ng, ds):
    nb, length, _ = zxbc3.shape
    di = nh * hd
    cdim = di + 2 * ng * ds
    ts = min(CHUNK, length)
    ns = length // ts
    nprev = prev[0].shape[0] if prev else 0
    body = functools.partial(_ssd_body, ts=ts, nh=nh, hd=hd, ng=ng, ds=ds, nprev=nprev)
    assert cdim == 2 * di
    dt_blk = (di + cdim) // LANES
    prev_specs = [
        pl.BlockSpec((nprev, 1, di, ds), lambda b, s: (0, b, 0, 0)),
        pl.BlockSpec((nprev, 1, CONV_W - 1, cdim), lambda b, s: (0, b, 0, 0)),
    ] if prev else []
    return pl.pallas_call(
        body,
        grid=(nb, ns),
        in_specs=[
            pl.BlockSpec((1, ts, di), lambda b, s: (b, s, 1)),
            pl.BlockSpec((1, ts, di), lambda b, s: (b, s, 2)),
            pl.BlockSpec((1, ts, LANES), lambda b, s: (b, s, dt_blk)),
            pl.BlockSpec((1, 1, CONV_W - 1, cdim), lambda b, s: (layer, b, 0, 0)),
            pl.BlockSpec((1, 1, di, ds), lambda b, s: (layer, b, 0, 0)),
            pl.BlockSpec((CONV_W, cdim), lambda b, s: (0, 0)),
            pl.BlockSpec((1, cdim), lambda b, s: (0, 0)),
            pl.BlockSpec((1, LANES), lambda b, s: (0, 0)),
            pl.BlockSpec((1, LANES), lambda b, s: (0, 0)),
            pl.BlockSpec((1, di), lambda b, s: (0, 0)),
            pl.BlockSpec((2 * LANES, di), lambda b, s: (0, 0)),
        ] + prev_specs,
        out_specs=[
            pl.BlockSpec((1, di // LANES, ts, LANES), lambda b, s: (b, 0, s, 0)),
            pl.BlockSpec((nprev + 1, 1, di, ds), lambda b, s: (0, b, 0, 0)),
            pl.BlockSpec((nprev + 1, 1, CONV_W - 1, cdim), lambda b, s: (0, b, 0, 0)),
        ],
        out_shape=[SDS((nb, di // LANES, length, LANES), F32), SDS((nprev + 1, nb, di, ds), F32),
                   SDS((nprev + 1, nb, CONV_W - 1, cdim), F32)],
        scratch_shapes=[
            pltpu.VMEM((cdim // LANES, 8 + CHUNK, LANES), F32),
            pltpu.VMEM((CHUNK, LANES), F32),
            pltpu.VMEM((di, ds), F32),
            pltpu.VMEM((2, CHUNK, 2 * LANES), BF16),
        ],
        compiler_params=_cparams(("parallel", "arbitrary")),
        name="ssd",
    )(zxbc3, zxbc3, zxbc3, conv0, h0, cw, cb, dtb, alog, dsk, emat, *prev)


def _post_body(y_ref, z_ref, x_ref, gw_ref, w_ref, o_ref, *, ng):
    if len(y_ref.shape) == 4:
        y = jnp.concatenate([y_ref[0, c] for c in range(y_ref.shape[1])], axis=1)
    else:
        y = y_ref[...]
    yz = y * _silu(z_ref[...])
    width = yz.shape[1] // ng
    parts = []
    for g in range(ng):
        blk = yz[:, g * width:(g + 1) * width]
        ms = jnp.mean(blk * blk, axis=-1, keepdims=True)
        parts.append(blk * lax.rsqrt(ms + EPS))
    yn = jnp.concatenate(parts, axis=1) * gw_ref[...]
    o_ref[...] = x_ref[...] + _dot(yn.astype(BF16), w_ref[0].astype(BF16))


def _post(y, zxbc2, x2, gw, w_out, layer, tm, ng):
    m, d = x2.shape
    if y.ndim == 4:
        nslab, length = y.shape[1], y.shape[2]
        di = nslab * LANES
        per_seq = length // tm
        y_spec = pl.BlockSpec((1, nslab, tm, LANES), lambda i: (i // per_seq, 0, i % per_seq, 0))
    else:
        di = y.shape[1]
        y_spec = pl.BlockSpec((tm, di), lambda i: (i, 0))
    return pl.pallas_call(
        functools.partial(_post_body, ng=ng),
        grid=(m // tm,),
        in_specs=[
            y_spec,
            pl.BlockSpec((tm, di), lambda i: (i, 0)),
            pl.BlockSpec((tm, d), lambda i: (i, 0)),
            pl.BlockSpec((1, di), lambda i: (0, 0)),
            pl.BlockSpec((1, di, d), lambda i: (layer, 0, 0)),
        ],
        out_specs=pl.BlockSpec((tm, d), lambda i: (i, 0)),
        out_shape=SDS((m, d), F32),
        compiler_params=_cparams(("parallel",)),
        name="post_mamba",
    )(y, zxbc2, x2, gw, w_out)


def _mlp_body(x_ref, g_ref, wu_ref, wd_ref, o_ref, u_scr):
    _mlp_prologue(x_ref, g_ref, u_scr)
    _mlp_epilogue(x_ref, o_ref, _mlp_main(wu_ref, wd_ref, u_scr))


def _mlp_prologue(x_ref, g_ref, u_scr):
    @pl.when(pl.program_id(1) == 0)
    def _():
        u_scr[...] = _rms(x_ref[...], g_ref[...]).astype(BF16)


def _mlp_main(wu_ref, wd_ref, u_scr):
    u = u_scr[...]
    half = wu_ref.shape[2] // 2
    hs = [jnp.maximum(_dot(u, wu_ref[0, :, k * half:(k + 1) * half].astype(BF16)), 0.0) for k in range(2)]
    part = _dot((hs[0] * hs[0]).astype(BF16), wd_ref[0, 0:half, :].astype(BF16))
    return part + _dot((hs[1] * hs[1]).astype(BF16), wd_ref[0, half:2 * half, :].astype(BF16))


def _mlp_epilogue(x_ref, o_ref, part):
    c = pl.program_id(1)

    @pl.when(c == 0)
    def _():
        o_ref[...] = x_ref[...] + part

    @pl.when(c != 0)
    def _():
        o_ref[...] += part


def _mlp(x2, g, w_up, w_down, layer, tm, tf):
    m, d = x2.shape
    ff = w_up.shape[2]
    return pl.pallas_call(
        _mlp_body,
        grid=(m // tm, ff // tf),
        in_specs=[
            pl.BlockSpec((tm, d), lambda i, c: (i, 0)),
            pl.BlockSpec((1, d), lambda i, c: (0, 0)),
            pl.BlockSpec((1, d, tf), lambda i, c: (layer, 0, c)),
            pl.BlockSpec((1, tf, d), lambda i, c: (layer, c, 0)),
        ],
        out_specs=pl.BlockSpec((tm, d), lambda i, c: (i, 0)),
        out_shape=SDS((m, d), F32),
        scratch_shapes=[pltpu.VMEM((tm, d), BF16)],
        compiler_params=pltpu.CompilerParams(dimension_semantics=("parallel", "arbitrary"),
                                             vmem_limit_bytes=MLP_VMEM_LIMIT),
        name="mlp",
    )(x2, g, w_up, w_down)


def _row_group_norm(xt, width):
    n, tm = xt.shape
    x3 = xt.reshape(n // width, width, tm)
    ms = jnp.mean(x3 * x3, axis=1, keepdims=True)
    return (x3 * lax.rsqrt(ms + EPS)).reshape(n, tm)


def _kv_body(x_ref, g_ref, wkt_ref, wv_ref, kn_ref, kt_ref, v_ref, *head_refs, nha, ahd):
    u = _rms(x_ref[0], g_ref[...]).astype(BF16)
    kt = _row_group_norm(_dot_nt(wkt_ref[...], u), ahd) * kn_ref[...]
    v = _dot(u, wv_ref[...])
    kt_ref[0] = kt
    v_ref[0] = v
    if head_refs:
        kh_ref, vht_ref = head_refs
        k = kt.T
        for h in range(nha):
            kh_ref[0, h] = k[:, h * LANES:(h + 1) * LANES].astype(BF16)
            vht_ref[0, h] = v[:, h * LANES:(h + 1) * LANES].T.astype(BF16)


def _kv(x3, g, wk_t, wv, kn_col, tm, nha, ahd, head_major):
    nb, t, d = x3.shape
    kd = nha * LANES
    out_specs = [
        pl.BlockSpec((1, kd, tm), lambda b, i: (b, 0, i)),
        pl.BlockSpec((1, tm, kd), lambda b, i: (b, i, 0)),
    ]
    out_shape = [SDS((nb, kd, t), F32), SDS((nb, t, kd), F32)]
    if head_major:
        out_specs += [
            pl.BlockSpec((1, nha, tm, LANES), lambda b, i: (b, 0, i, 0)),
            pl.BlockSpec((1, nha, LANES, tm), lambda b, i: (b, 0, 0, i)),
        ]
        out_shape += [SDS((nb, nha, t, LANES), BF16), SDS((nb, nha, LANES, t), BF16)]
    return pl.pallas_call(
        functools.partial(_kv_body, nha=nha, ahd=ahd),
        grid=(nb, t // tm),
        in_specs=[
            pl.BlockSpec((1, tm, d), lambda b, i: (b, i, 0)),
            pl.BlockSpec((1, d), lambda b, i: (0, 0)),
            pl.BlockSpec((kd, d), lambda b, i: (0, 0)),
            pl.BlockSpec((d, kd), lambda b, i: (0, 0)),
            pl.BlockSpec((kd, 1), lambda b, i: (0, 0)),
        ],
        out_specs=out_specs,
        out_shape=out_shape,
        compiler_params=_cparams(("parallel", "parallel")),
        name="kv_proj",
    )(x3, g, wk_t, wv, kn_col)


def _q_body(x_ref, g_ref, w_ref, qn_ref, gm_ref, q_ref, *, scale):
    u = _rms(x_ref[0], g_ref[...]).astype(BF16)
    qn = _group_norm64(_dot(u, w_ref[0].astype(BF16)), gm_ref[...]) * qn_ref[...] * scale
    q_ref[0] = qn.astype(BF16)


def _qproj(x3, g, w_q, layer, qn_t, gmat, tm, nha, scale):
    nb, t, d = x3.shape
    kd = nha * LANES
    return pl.pallas_call(
        functools.partial(_q_body, scale=scale),
        grid=(nb, t // tm),
        in_specs=[
            pl.BlockSpec((1, tm, d), lambda b, i: (b, i, 0)),
            pl.BlockSpec((1, d), lambda b, i: (0, 0)),
            pl.BlockSpec((1, d, kd), lambda b, i: (layer, 0, 0)),
            pl.BlockSpec((1, kd), lambda b, i: (0, 0)),
            pl.BlockSpec((256, 256), lambda b, i: (0, 0)),
        ],
        out_specs=pl.BlockSpec((1, tm, kd), lambda b, i: (b, i, 0)),
        out_shape=SDS((nb, t, kd), BF16),
        compiler_params=_cparams(("parallel", "parallel")),
        name="q_proj",
    )(x3, g, w_q, qn_t, gmat)


def _lam(l1, k1, l2, k2, lam_init):
    return (jnp.exp(jnp.sum(l1 * k1, axis=-1, keepdims=True))
            - jnp.exp(jnp.sum(l2 * k2, axis=-1, keepdims=True)) + lam_init)


def _attn_body(it_ref, jt_ref, x_ref, g_ref, wqt_ref, qn_ref, k_ref, vt_ref,
               l1_ref, k1_ref, l2_ref, k2_ref, sw_ref, wo_ref, y_ref,
               qt_ref, m_scr, l_scr, acc_scr, st_scr, mx_scr, *, tq, tk, nha, ahd, qscale, lam_init):
    t = pl.program_id(1)
    i = it_ref[t]
    j = jt_ref[t]

    @pl.when(j == 0)
    def _():
        m_scr[...] = jnp.full_like(m_scr, NEG)
        l_scr[...] = jnp.zeros_like(l_scr)
        acc_scr[...] = jnp.zeros_like(acc_scr)
        u = _rms(x_ref[0], g_ref[...]).astype(BF16)
        qt = _row_group_norm(_dot_nt(wqt_ref[0], u), ahd) * (qn_ref[...] * qscale)
        rows = lax.broadcasted_iota(jnp.int32, (LANES, tq), 0)
        for h in range(nha):
            blk = qt[h * LANES:(h + 1) * LANES, :]
            qt_ref[h, 0] = jnp.where(rows < ahd, blk, 0.0).astype(BF16)
            qt_ref[h, 1] = jnp.where(rows >= ahd, blk, 0.0).astype(BF16)

    def scores(h, c, slot, bias):
        st = _dot(k_ref[0, h], qt_ref[h, c])
        if bias is not None:
            st = st + bias
        st_scr[slot] = st
        mx_scr[slot] = jnp.max(st, axis=0, keepdims=True)

    def accumulate(h, c, slot):
        m_prev = m_scr[h, c]
        m_new = jnp.maximum(m_prev, mx_scr[slot])
        alpha = jnp.exp2(m_prev - m_new)
        p = jnp.exp2(st_scr[slot] - m_new)
        l_scr[h, c] = alpha * l_scr[h, c] + jnp.sum(p, axis=0, keepdims=True)
        acc_scr[h, c] = alpha * acc_scr[h, c] + _dot(vt_ref[0, h], p.astype(BF16))
        m_scr[h, c] = m_new

    def run_heads(bias):
        scores(0, 0, 0, bias)

        def head(h, carry):
            scores(h, 1, 1, bias)
            accumulate(h, 0, 0)
            scores(jnp.minimum(h + 1, nha - 1), 0, 0, bias)
            accumulate(h, 1, 1)
            return carry

        lax.fori_loop(0, nha, head, 0, unroll=4)

    needs_mask = (j + 1) * tk - 1 > i * tq

    @pl.when(needs_mask)
    def _():
        kpos = j * tk + lax.broadcasted_iota(jnp.int32, (tk, tq), 0)
        qpos = i * tq + lax.broadcasted_iota(jnp.int32, (tk, tq), 1)
        run_heads(jnp.where(kpos <= qpos, 0.0, NEG))

    @pl.when(jnp.logical_not(needs_mask))
    def _():
        run_heads(None)

    @pl.when((j + 1) * tk >= (i + 1) * tq)
    def _():
        lam = _lam(l1_ref[...], k1_ref[...], l2_ref[...], k2_ref[...], lam_init)
        parts = []
        for h in range(nha):
            ot = (acc_scr[h, 0] * (1.0 / l_scr[h, 0])
                  - (lam * (1.0 / l_scr[h, 1])) * acc_scr[h, 1])
            ms = jnp.mean(ot * ot, axis=0, keepdims=True)
            parts.append((ot * lax.rsqrt(ms + EPS)).T)
        on = jnp.concatenate(parts, axis=1) * (sw_ref[...] * (1.0 - lam_init))
        y_ref[0] = x_ref[0] + _dot(on.astype(BF16), wo_ref[0].astype(BF16))


def _attn_prompt(x3, g, wq_t, qn_col, kh, vht, l1, k1, l2, k2, sw_t, w_o, layer, tq, tk, ahd, qscale,
                 lam_init):
    nb, t, d = x3.shape
    nha = kh.shape[1]
    kd = nha * LANES
    hd = ahd
    its, jts = [], []
    for i in range(t // tq):
        for j in range(((i + 1) * tq + tk - 1) // tk):
            its.append(i)
            jts.append(j)
    it = jnp.asarray(its, jnp.int32)
    jt = jnp.asarray(jts, jnp.int32)
    lspec = pl.BlockSpec((1, hd), lambda b, s, it, jt: (0, 0))
    grid_spec = pltpu.PrefetchScalarGridSpec(
        num_scalar_prefetch=2,
        grid=(nb, len(its)),
        in_specs=[
            pl.BlockSpec((1, tq, d), lambda b, s, it, jt: (b, it[s], 0)),
            pl.BlockSpec((1, d), lambda b, s, it, jt: (0, 0)),
            pl.BlockSpec((1, kd, d), lambda b, s, it, jt: (layer, 0, 0)),
            pl.BlockSpec((kd, 1), lambda b, s, it, jt: (0, 0)),
            pl.BlockSpec((1, nha, tk, LANES), lambda b, s, it, jt: (b, 0, jt[s], 0)),
            pl.BlockSpec((1, nha, LANES, tk), lambda b, s, it, jt: (b, 0, 0, jt[s])),
            lspec, lspec, lspec, lspec,
            pl.BlockSpec((1, kd), lambda b, s, it, jt: (0, 0)),
            pl.BlockSpec((1, kd, d), lambda b, s, it, jt: (layer, 0, 0)),
        ],
        out_specs=pl.BlockSpec((1, tq, d), lambda b, s, it, jt: (b, it[s], 0)),
        scratch_shapes=[
            pltpu.VMEM((nha, 2, LANES, tq), BF16),
            pltpu.VMEM((nha, 2, 1, tq), F32),
            pltpu.VMEM((nha, 2, 1, tq), F32),
            pltpu.VMEM((nha, 2, LANES, tq), F32),
            pltpu.VMEM((2, tk, tq), F32),
            pltpu.VMEM((2, 1, tq), F32),
        ],
    )
    return pl.pallas_call(
        functools.partial(_attn_body, tq=tq, tk=tk, nha=nha, ahd=ahd, qscale=qscale, lam_init=lam_init),
        grid_spec=grid_spec,
        out_shape=SDS((nb, t, d), F32),
        compiler_params=_cparams(("parallel", "arbitrary")),
        name="attn_prompt",
    )(it, jt, x3, g, wq_t, qn_col, kh, vht, l1, k1, l2, k2, sw_t, w_o)


def _decode_body(pt_ref, q_ref, kn_ref, vn_ref, l1_ref, k1_ref, l2_ref, k2_ref, *rest,
                 pps, tn, nha, hd, lam_init):
    _decode_step(pl.program_id(0), pl.program_id(1), pl.num_programs(1), q_ref, kn_ref, vn_ref,
                 (l1_ref, k1_ref, l2_ref, k2_ref), rest[:pps], rest[pps:2 * pps], rest[2 * pps],
                 rest[2 * pps + 1:], tn=tn, nha=nha, hd=hd, lam_init=lam_init)


def _decode_step(b_idx, j, nj, q_ref, kn_ref, vn_ref, lam_refs, k_refs, v_refs, o_ref, scratch,
                 *, tn, nha, hd, lam_init, phases=("setup", "pages", "finish")):
    l1_ref, k1_ref, l2_ref, k2_ref = lam_refs
    qb, m_scr, l_scr, acc_scr = scratch
    pps = len(k_refs)
    nmap = 2 * nha
    nrow = tn * nmap
    kd = nha * LANES
    page = k_refs[0].shape[2]

    def online(sc, vals):
        n = len(vals)
        m_prev = m_scr[...]
        blk_max = sc[:, 0:LANES]
        for b in range(1, n):
            blk_max = jnp.maximum(blk_max, sc[:, b * LANES:(b + 1) * LANES])
        m_new = jnp.maximum(m_prev, jnp.max(blk_max, axis=-1, keepdims=True))
        alpha = jnp.exp(m_prev - m_new)
        p = jnp.exp(sc - jnp.concatenate([m_new] * n, axis=1))
        psum = p[:, 0:LANES]
        for b in range(1, n):
            psum = psum + p[:, b * LANES:(b + 1) * LANES]
        l_scr[...] = alpha * l_scr[...] + psum
        pv = _dot(p[:, 0:LANES].astype(BF16), vals[0])
        for b in range(1, n):
            pv = pv + _dot(p[:, b * LANES:(b + 1) * LANES].astype(BF16), vals[b])
        acc_scr[...] = jnp.concatenate([alpha] * (kd // LANES), axis=1) * acc_scr[...] + pv
        m_scr[...] = m_new

    def page_values(v_ref):
        return jnp.concatenate(
            [v_ref[0, pl.ds(h, page, stride=nha), :] for h in range(nha)], axis=1).astype(BF16)

    def when(phase, cond):
        return pl.when(cond) if phase in phases else (lambda fn: None)

    @when("setup", j == 0)
    def _():
        rid = lax.broadcasted_iota(jnp.int32, (nmap, kd), 0)
        lid = lax.broadcasted_iota(jnp.int32, (nmap, kd), 1)
        sel = (lid >= rid * hd) & (lid < (rid + 1) * hd)
        qf = q_ref[0].astype(F32)
        for tt in range(tn):
            rowq = jnp.broadcast_to(qf[tt:tt + 1, :], (nmap, kd))
            qb[tt * nmap:(tt + 1) * nmap, :] = jnp.where(sel, rowq, 0.0).astype(BF16)
        m_scr[...] = jnp.full_like(m_scr, NEG)
        l_scr[...] = jnp.zeros_like(l_scr)
        acc_scr[...] = jnp.zeros_like(acc_scr)
        nnew = kn_ref.shape[2]
        qrow = lax.broadcasted_iota(jnp.int32, (nrow, LANES), 0)
        ktok = lax.broadcasted_iota(jnp.int32, (nrow, LANES), 1) - b_idx * tn
        for blk in range(nnew // LANES):
            sc = _dot(qb[...], kn_ref[0, :, blk * LANES:(blk + 1) * LANES].astype(BF16))
            kt = ktok - blk * LANES
            sc = jnp.where((kt >= 0) & (kt * nmap <= qrow), sc, NEG)
            online(sc, [vn_ref[0, blk * LANES:(blk + 1) * LANES, :].astype(BF16)])

    if "pages" in phases:
        scs = [_dot(qb[...], k_refs[b][0].astype(BF16)) for b in range(pps)]
        online(jnp.concatenate(scs, axis=1), [page_values(v_refs[b]) for b in range(pps)])

    @when("finish", j == nj - 1)
    def _():
        lam = _lam(l1_ref[...], k1_ref[...], l2_ref[...], k2_ref[...], lam_init)
        inv_l = 1.0 / jnp.sum(l_scr[...], axis=-1, keepdims=True)
        rid = lax.broadcasted_iota(jnp.int32, (nmap, 1), 0)
        rid2 = lax.broadcasted_iota(jnp.int32, (nmap, kd), 0)
        lid2 = lax.broadcasted_iota(jnp.int32, (nmap, kd), 1)
        own = (lid2 >= (rid2 >> 1) * LANES) & (lid2 < ((rid2 >> 1) + 1) * LANES)
        for tt in range(tn):
            il = inv_l[tt * nmap:(tt + 1) * nmap, :]
            coef = jnp.where((rid & 1) == 0, il, -lam * il)
            w = jnp.where(own, acc_scr[tt * nmap:(tt + 1) * nmap, :] * coef, 0.0)
            o_ref[0, tt:tt + 1, :] = jnp.sum(w, axis=0, keepdims=True)


def _attn_sample(page_table, q3, kt_new, v_new, cache_kt, cache_v2, l1, k1, l2, k2, pps, hd, lam_init):
    nb, tn, kd = q3.shape
    nha = kd // LANES
    n_pages = page_table.shape[1]
    page = cache_kt.shape[2]
    nnew = kt_new.shape[2]
    nrow = tn * 2 * nha
    assert page == LANES and nnew == LANES and nb * tn == nnew
    pt_flat = page_table.reshape(-1)

    def k_spec(b_off):
        return pl.BlockSpec((1, kd, page),
                            lambda b, j, pt: (pt[b * n_pages + j * pps + b_off], 0, 0))

    def v_spec(b_off):
        return pl.BlockSpec((1, page * nha, LANES),
                            lambda b, j, pt: (pt[b * n_pages + j * pps + b_off], 0, 0))

    lspec = pl.BlockSpec((1, hd), lambda b, j, pt: (0, 0))
    grid_spec = pltpu.PrefetchScalarGridSpec(
        num_scalar_prefetch=1,
        grid=(nb, n_pages // pps),
        in_specs=[
            pl.BlockSpec((1, tn, kd), lambda b, j, pt: (b, 0, 0)),
            pl.BlockSpec((1, kd, nnew), lambda b, j, pt: (0, 0, 0)),
            pl.BlockSpec((1, nnew, kd), lambda b, j, pt: (0, 0, 0)),
            lspec, lspec, lspec, lspec]
        + [k_spec(b) for b in range(pps)] + [v_spec(b) for b in range(pps)],
        out_specs=pl.BlockSpec((1, tn, kd), lambda b, j, pt: (b, 0, 0)),
        scratch_shapes=[
            pltpu.VMEM((nrow, kd), BF16),
            pltpu.VMEM((nrow, LANES), F32),
            pltpu.VMEM((nrow, LANES), F32),
            pltpu.VMEM((nrow, kd), F32),
        ],
    )
    return pl.pallas_call(
        functools.partial(_decode_body, pps=pps, tn=tn, nha=nha, hd=hd, lam_init=lam_init),
        grid_spec=grid_spec,
        out_shape=SDS((nb, tn, kd), F32),
        compiler_params=_cparams(("parallel", "arbitrary")),
        name="attn_sample",
    )(pt_flat, q3, kt_new, v_new, l1, k1, l2, k2, *([cache_kt] * pps), *([cache_v2] * pps))


def _mlp_decode_body(pt_ref, x_ref, g_ref, wu_ref, wd_ref, q_ref, kn_ref, vn_ref,
                     l1_ref, k1_ref, l2_ref, k2_ref, *rest, pps, spp, seq0, tn, nha, hd, lam_init):
    k_refs = rest[:pps]
    v_refs = rest[pps:2 * pps]
    o_ref, a_ref = rest[2 * pps], rest[2 * pps + 1]
    u_scr = rest[2 * pps + 2]
    step = pl.program_id(0) * pl.num_programs(1) + pl.program_id(1)
    decode = functools.partial(
        _decode_step, seq0 + step // spp, step % spp, spp, q_ref, kn_ref, vn_ref,
        (l1_ref, k1_ref, l2_ref, k2_ref), k_refs, v_refs, a_ref, rest[2 * pps + 3:],
        tn=tn, nha=nha, hd=hd, lam_init=lam_init)
    _mlp_prologue(x_ref, g_ref, u_scr)
    decode(phases=("setup",))
    part = _mlp_main(wu_ref, wd_ref, u_scr)
    decode(phases=("pages",))
    _mlp_epilogue(x_ref, o_ref, part)
    decode(phases=("finish",))


def _mlp_decode(x2, g, w_up, w_down, layer, tm, tf, page_table, q3, kt_new, v_new, cache_kt, cache_v2,
                l1, k1, l2, k2, seq0, nseq, pps, hd, lam_init):
    m, d = x2.shape
    ff = w_up.shape[2]
    _, tn, kd = q3.shape
    nha = kd // LANES
    n_pages = page_table.shape[1]
    page = cache_kt.shape[2]
    nnew = kt_new.shape[2]
    nrow = tn * 2 * nha
    spp = n_pages // pps
    nc = ff // tf
    assert (m // tm) * nc == nseq * spp and page == LANES and nnew == LANES
    pt_flat = page_table.reshape(-1)

    def seq_of(i, c):
        return seq0 + (i * nc + c) // spp

    def page_of(i, c, pt, b_off):
        return pt[seq_of(i, c) * n_pages + ((i * nc + c) % spp) * pps + b_off]

    def k_spec(b_off):
        return pl.BlockSpec((1, kd, page), lambda i, c, pt: (page_of(i, c, pt, b_off), 0, 0))

    def v_spec(b_off):
        return pl.BlockSpec((1, page * nha, LANES), lambda i, c, pt: (page_of(i, c, pt, b_off), 0, 0))

    lspec = pl.BlockSpec((1, hd), lambda i, c, pt: (0, 0))
    grid_spec = pltpu.PrefetchScalarGridSpec(
        num_scalar_prefetch=1,
        grid=(m // tm, nc),
        in_specs=[
            pl.BlockSpec((tm, d), lambda i, c, pt: (i, 0)),
            pl.BlockSpec((1, d), lambda i, c, pt: (0, 0)),
            pl.BlockSpec((1, d, tf), lambda i, c, pt: (layer, 0, c)),
            pl.BlockSpec((1, tf, d), lambda i, c, pt: (layer, c, 0)),
            pl.BlockSpec((1, tn, kd), lambda i, c, pt: (seq_of(i, c), 0, 0)),
            pl.BlockSpec((1, kd, nnew), lambda i, c, pt: (0, 0, 0)),
            pl.BlockSpec((1, nnew, kd), lambda i, c, pt: (0, 0, 0)),
            lspec, lspec, lspec, lspec]
        + [k_spec(b) for b in range(pps)] + [v_spec(b) for b in range(pps)],
        out_specs=[
            pl.BlockSpec((tm, d), lambda i, c, pt: (i, 0)),
            pl.BlockSpec((1, tn, kd), lambda i, c, pt: (seq_of(i, c) - seq0, 0, 0)),
        ],
        scratch_shapes=[
            pltpu.VMEM((tm, d), BF16),
            pltpu.VMEM((nrow, kd), BF16),
            pltpu.VMEM((nrow, LANES), F32),
            pltpu.VMEM((nrow, LANES), F32),
            pltpu.VMEM((nrow, kd), F32),
        ],
    )
    return pl.pallas_call(
        functools.partial(_mlp_decode_body, pps=pps, spp=spp, seq0=seq0, tn=tn, nha=nha, hd=hd,
                          lam_init=lam_init),
        grid_spec=grid_spec,
        out_shape=[SDS((m, d), F32), SDS((nseq, tn, kd), F32)],
        compiler_params=pltpu.CompilerParams(dimension_semantics=("arbitrary", "arbitrary"),
                                             vmem_limit_bytes=MLP_VMEM_LIMIT),
        name="mlp_decode",
    )(pt_flat, x2, g, w_up, w_down, q3, kt_new, v_new, l1, k1, l2, k2,
      *([cache_kt] * pps), *([cache_v2] * pps))


def _attn_out_body(o_ref, x_ref, sw_ref, w_ref, y_ref, *, nha, post_scale):
    o = o_ref[...]
    parts = []
    for h in range(nha):
        blk = o[:, h * LANES:(h + 1) * LANES]
        ms = jnp.mean(blk * blk, axis=-1, keepdims=True)
        parts.append(blk * lax.rsqrt(ms + EPS))
    on = jnp.concatenate(parts, axis=1) * sw_ref[...] * post_scale
    y_ref[...] = x_ref[...] + _dot(on.astype(BF16), w_ref[0].astype(BF16))


def _attn_out(o2, x2, sw_t, w_o, layer, tm, nha, post_scale):
    m, d = x2.shape
    vd = o2.shape[1]
    return pl.pallas_call(
        functools.partial(_attn_out_body, nha=nha, post_scale=post_scale),
        grid=(m // tm,),
        in_specs=[
            pl.BlockSpec((tm, vd), lambda i: (i, 0)),
            pl.BlockSpec((tm, d), lambda i: (i, 0)),
            pl.BlockSpec((1, vd), lambda i: (0, 0)),
            pl.BlockSpec((1, vd, d), lambda i: (layer, 0, 0)),
        ],
        out_specs=pl.BlockSpec((tm, d), lambda i: (i, 0)),
        out_shape=SDS((m, d), F32),
        compiler_params=_cparams(("parallel",)),
        name="attn_out",
    )(o2, x2, sw_t, w_o)


def kernel(x_prompt, x_sample, state_ssm, state_conv, cache_k, cache_v, page_table, norm_a, w_in_a, conv_w, conv_b, dt_bias, a_log, d_skip, gnorm_w, w_out_a, kv_norm, w_kv, k_norm, norm_b, w_q, q_norm, lam_q1, lam_k1, lam_q2, lam_k2, subln, w_o, norm_m, w_up, w_down):
    n_a = norm_a.shape[0]
    n_b = norm_b.shape[0]
    d = x_prompt.shape[-1]
    nh, hd, ds = state_ssm.shape[2], state_ssm.shape[3], state_ssm.shape[4]
    di = nh * hd
    cdim = state_conv.shape[-1]
    ng = (cdim - di) // (2 * ds)
    nha, ahd = cache_k.shape[2], cache_k.shape[4]
    kd = nha * 2 * ahd
    page = cache_k.shape[1]
    assert 2 * ahd == LANES and cache_v.shape[3] == LANES and nh <= LANES

    emat = (jnp.arange(LANES)[:, None] == (jnp.arange(di)[None, :] // hd)).astype(BF16)
    emat = jnp.concatenate([emat, emat], axis=0)
    gidx = jnp.arange(256) // ahd
    gmat = jnp.where(gidx[:, None] == gidx[None, :], 1.0 / ahd, 0.0).astype(BF16)

    def row(v):
        return v.reshape(1, -1).astype(F32)

    def pad_lanes(v):
        return jnp.pad(v.astype(F32), (0, LANES - v.shape[0])).reshape(1, LANES)

    n_in = di + cdim + nh
    tn_in = -(-n_in // (2 * LANES)) * LANES
    w_in_b = jnp.pad(w_in_a, ((0, 0), (0, 0), (0, 2 * tn_in - n_in))).astype(BF16)
    wk_t_b = w_kv[:, :kd].T.astype(BF16)
    wv_b = w_kv[:, kd:].astype(BF16)
    wq_t_b = jnp.swapaxes(w_q, 1, 2).astype(BF16)
    scale = ahd ** -0.5

    def trunk(x3, conv0, ssm0, seq_major):
        nb, t, _ = x3.shape
        m = nb * t
        tm = min(512, m)
        x2 = x3.reshape(m, d)
        ab, at = (nb, t) if seq_major else (1, m)
        tma = min(512, at)
        states = ()
        k_new = v_new = kv_ctx = None
        for l in range(n_a + n_b):
            if l < n_a:
                zxbc = _inproj(x2, row(norm_a[l]), w_in_b, l, tm, tn_in)
                y3, hn, cn = _ssd(
                    zxbc.reshape(nb, t, 2 * tn_in), conv0, ssm0.reshape(n_a, nb, di, ds), l,
                    conv_w[l], row(conv_b[l]), pad_lanes(dt_bias[l]), pad_lanes(a_log[l]),
                    row(jnp.repeat(d_skip[l], hd)), emat, states, nh=nh, hd=hd, ng=ng, ds=ds)
                states = (hn, cn)
                tmp = min(256, m)
                if t % tmp:
                    y3 = jnp.transpose(y3, (0, 2, 1, 3)).reshape(m, di)
                x2 = _post(y3, zxbc, x2, row(gnorm_w[l]), w_out_a, l, tmp, ng)
            else:
                jb = l - n_a
                if l == n_a:
                    kv_ctx = _kv(x2.reshape(ab, at, d), row(kv_norm), wk_t_b, wv_b,
                                 jnp.tile(k_norm, 2 * nha).reshape(kd, 1).astype(F32), tma, nha, ahd, seq_major)
                    k_new = jnp.transpose(kv_ctx[0].reshape(ab, nha, 2, ahd, at), (0, 4, 1, 2, 3))
                    k_new = k_new.reshape(nb, t, nha, 2, ahd)
                    v_new = kv_ctx[1].reshape(nb, t, nha, LANES)
                lam_init = 0.8 - 0.6 * math.exp(-0.3 * l)
                lams = (row(lam_q1[jb]), row(lam_k1[jb]), row(lam_q2[jb]), row(lam_k2[jb]))
                qn_t = jnp.tile(q_norm[jb], 2 * nha).astype(F32)
                sw_t = row(jnp.tile(subln[jb], nha))
                if seq_major:
                    x2 = _attn_prompt(x2.reshape(nb, t, d), row(norm_b[jb]), wq_t_b, qn_t.reshape(kd, 1),
                                      kv_ctx[2], kv_ctx[3], *lams, sw_t, w_o, jb, 512, 512, ahd,
                                      scale * LOG2E, lam_init).reshape(m, d)
                else:
                    q3 = _qproj(x2.reshape(ab, at, d), row(norm_b[jb]), w_q, jb,
                                qn_t.reshape(1, kd), gmat, tma, nha, scale)
                    o3 = yield ("attend", (q3.reshape(nb, t, kd), kv_ctx[0], kv_ctx[1], lams, lam_init))
                    x2 = _attn_out(o3.reshape(m, kd), x2, sw_t, w_o, jb, tm, nha, 1.0 - lam_init)
            if seq_major:
                x2 = yield ("mlp", (x2, row(norm_m[l]), l))
            else:
                x2 = _mlp(x2, row(norm_m[l]), w_up, w_down, l, min(1024, m), 1024)
        return x2.reshape(nb, t, d), states[0].reshape(n_a, nb, nh, hd, ds), states[1], k_new, v_new

    n_pool = cache_k.shape[0]
    cache_kt = jnp.transpose(cache_k, (0, 2, 3, 4, 1)).reshape(n_pool, kd, page)
    cache_v2 = cache_v.reshape(n_pool, page * nha, LANES)

    bp = x_prompt.shape[0]
    conv0_p = jnp.zeros((n_a, bp, CONV_W - 1, cdim), F32)
    ssm0_p = jnp.zeros((n_a, bp, nh, hd, ds), F32)
    prompt = trunk(x_prompt, conv0_p, ssm0_p, True)
    sample = trunk(x_sample, state_conv, state_ssm, False)

    m_p = bp * x_prompt.shape[1]
    nb_s, n_pages = page_table.shape
    share = (n_a + n_b) // n_b
    tm_f, tf_f, pps_f = 1024, 512, 8
    w_up_b, w_down_b = w_up.astype(BF16), w_down.astype(BF16)
    fused = ((n_a + n_b) % n_b == 0 and nb_s % share == 0 and m_p % tm_f == 0 and n_pages % pps_f == 0
             and (m_p // tm_f) * (w_up.shape[2] // tf_f) == (nb_s // share) * (n_pages // pps_f))

    def advance(gen, value):
        try:
            return gen.send(value), None
        except StopIteration as stop:
            return None, stop.value

    req_s, out_s = advance(sample, None)
    req_p, out_p = advance(prompt, None)
    while req_p is not None or req_s is not None:
        if fused and req_p is not None and req_s is not None:
            q3, kt_new, v_new, lams, lam_init = req_s[1]
            nseq = nb_s // share
            parts = []
            for k in range(share):
                x2, g_m, l = req_p[1]
                x2, o_part = _mlp_decode(x2, g_m, w_up_b, w_down_b, l, tm_f, tf_f, page_table, q3, kt_new, v_new,
                                         cache_kt, cache_v2, *lams, k * nseq, nseq, pps_f, ahd, lam_init)
                parts.append(o_part)
                req_p, out_p = advance(prompt, x2)
            req_s, out_s = advance(sample, jnp.concatenate(parts, axis=0))
        elif req_p is not None:
            x2, g_m, l = req_p[1]
            req_p, out_p = advance(prompt, _mlp(x2, g_m, w_up, w_down, l, min(1024, m_p), 1024))
        else:
            q3, kt_new, v_new, lams, lam_init = req_s[1]
            req_s, out_s = advance(sample, _attn_sample(page_table, q3, kt_new, v_new, cache_kt, cache_v2,
                                                        *lams, 16, ahd, lam_init))
    y_p, ssm_p, conv_p, k_p, v_p = out_p
    y_s, ssm_s, conv_s, k_s, v_s = out_s
    return (y_p, y_s, ssm_p, conv_p, k_p, v_p, ssm_s, conv_s, k_s, v_s)
```

```python
import functools
import math

import jax
import jax.numpy as jnp
from jax import lax
from jax.experimental import pallas as pl
from jax.experimental.pallas import tpu as pltpu

F32 = jnp.float32
BF16 = jnp.bfloat16
EPS = 1e-6
LANES = 128
CHUNK = 128
CONV_W = 4
VMEM_LIMIT = 48 * 1024 * 1024
MLP_VMEM_LIMIT = 56 * 1024 * 1024
NEG = -1e30
LOG2E = math.log2(math.e)
SDS = jax.ShapeDtypeStruct


def _cparams(sem):
    return pltpu.CompilerParams(dimension_semantics=sem, vmem_limit_bytes=VMEM_LIMIT)


def _rms(x, g):
    ms = jnp.mean(x * x, axis=-1, keepdims=True)
    return x * lax.rsqrt(ms + EPS) * g


def _silu(x):
    return x * (1.0 / (1.0 + jnp.exp2(x * -LOG2E)))


def _softplus(x):
    e = jnp.exp(-jnp.abs(x))
    u = 1.0 + e
    log1p_e = jnp.where(u == 1.0, e, jnp.log(u) * (e / (u - 1.0)))
    return jnp.maximum(x, 0.0) + log1p_e


def _dot(a, b):
    return jnp.dot(a, b, preferred_element_type=F32)


def _dot_nt(a, b):
    return lax.dot_general(a, b, (((1,), (1,)), ((), ())), preferred_element_type=F32)


def _split_bf16(x):
    hi = x.astype(BF16)
    lo = (x - hi.astype(F32)).astype(BF16)
    return hi, lo


def _group_norm64(x, gmat):
    parts = []
    for c in range(x.shape[1] // 256):
        blk = x[:, c * 256:(c + 1) * 256]
        ms = _dot((blk * blk).astype(BF16), gmat)
        parts.append(blk * lax.rsqrt(ms + EPS))
    return jnp.concatenate(parts, axis=1)


def _inproj_body(x_ref, g_ref, w_ref, o_ref):
    u = _rms(x_ref[...], g_ref[...]).astype(BF16)
    o_ref[...] = _dot(u, w_ref[0])


def _inproj(x2, g, w_all, layer, tm, tn):
    m, d = x2.shape
    n = w_all.shape[2]
    return pl.pallas_call(
        _inproj_body,
        grid=(n // tn, m // tm),
        in_specs=[
            pl.BlockSpec((tm, d), lambda j, i: (i, 0)),
            pl.BlockSpec((1, d), lambda j, i: (0, 0)),
            pl.BlockSpec((1, d, tn), lambda j, i: (layer, 0, j)),
        ],
        out_specs=pl.BlockSpec((tm, tn), lambda j, i: (i, j)),
        out_shape=SDS((m, n), F32),
        compiler_params=_cparams(("parallel", "parallel")),
        name="inproj",
    )(x2, g, w_all)


def _ssd_body(xa_ref, xb_ref, dt_ref, conv0_ref, h0_ref, cw_ref, cb_ref, dtb_ref, alog_ref,
              dsk_ref, e_ref, *rest, ts, nh, hd, ng, ds, nprev):
    prev_refs, (y_ref, hn_ref, convn_ref, xpad, dtpad, hst, wsp) = rest[:len(rest) - 7], rest[-7:]
    q = CHUNK
    nph = q // 8
    di = nh * hd
    gn = ng * ds
    hpg = nh // ng
    gw = hpg * hd
    nxs = di // LANES
    ncs = xpad.shape[0]
    pad = CONV_W - 1
    s = pl.program_id(1)

    @pl.when(s == 0)
    def _():
        xpad[...] = jnp.zeros_like(xpad)
        dtpad[...] = jnp.zeros_like(dtpad)
        for c in range(ncs):
            xpad[c, 8 - pad:8, :] = conv0_ref[0, 0, :, c * LANES:(c + 1) * LANES]
        hst[...] = h0_ref[0, 0]

    for c in range(ncs):
        src = xa_ref if c < nxs else xb_ref
        cc0 = (c % nxs) * LANES
        xpad[c, 8:8 + ts, :] = src[0, :, cc0:cc0 + LANES]
        convn_ref[nprev, 0, :, c * LANES:(c + 1) * LANES] = xpad[c, 8 + ts - pad:8 + ts, :]
    dtpad[0:ts, :] = dt_ref[0]

    def phase_rows(ref2d, r0):
        return jnp.concatenate([ref2d[pl.ds(r0 + ph, nph, stride=8), :] for ph in range(8)], axis=0)

    def conv_silu(c0, width):
        slabs = []
        for c in range(c0 // LANES, (c0 + width) // LANES):
            acc = cb_ref[:, c * LANES:(c + 1) * LANES]
            for k in range(CONV_W):
                acc = acc + cw_ref[k:k + 1, c * LANES:(c + 1) * LANES] * phase_rows(xpad.at[c], 8 - pad + k)
            slabs.append(_silu(acc))
        return slabs[0] if len(slabs) == 1 else jnp.concatenate(slabs, axis=1)

    def real_time(pos):
        return ((pos & (nph - 1)) << 3) | (pos >> (nph.bit_length() - 1))

    def front(g):
        xg = conv_silu(g * gw, gw)
        bb = conv_silu(di + g * ds, ds).astype(BF16)
        cc = conv_silu(di + gn + g * ds, ds).astype(BF16)
        cbm = _dot_nt(cc, bb)
        hg = hst[g * gw:(g + 1) * gw, :]
        yoff = _dot_nt(cc, hg.astype(BF16))
        e_g = e_ref[:, g * gw:(g + 1) * gw]
        w1g = _dot(wsp[0], e_g)
        w2g = _dot(wsp[1], e_g)
        return xg, bb, cbm, hg, yoff, w1g, w2g

    dt = _softplus(phase_rows(dtpad, 0) + dtb_ref[...])
    if ts < q:
        dt = jnp.where(real_time(lax.broadcasted_iota(jnp.int32, (q, LANES), 0)) < ts, dt, 0.0)
    a = dt * (-jnp.exp(alog_ref[...]))
    causal = (real_time(lax.broadcasted_iota(jnp.int32, (q, q), 0))
              >= real_time(lax.broadcasted_iota(jnp.int32, (q, q), 1)))
    tril = jnp.where(causal, 1.0, 0.0).astype(BF16)

    a_hi, a_lo = _split_bf16(a)
    acs = _dot(jnp.concatenate([tril, tril], axis=1),
               jnp.concatenate([a_hi, a_lo], axis=0))
    acs2 = acs * LOG2E
    acs2_t = acs2.T
    dt_t = dt.T
    acs_last = acs[q - 1:q, :]
    wsp[0] = jnp.concatenate(_split_bf16(dt * jnp.exp(acs_last - acs)), axis=1)
    wsp[1] = jnp.concatenate(_split_bf16(jnp.exp(acs)), axis=1)
    cd = jnp.exp(acs_last)
    lane_g = lax.broadcasted_iota(jnp.int32, (1, gw), 1)

    nxt = front(0)
    for g in range(ng):
        xg, bb, cbm, hg, yoff, w1g, w2g = nxt
        if g + 1 < ng:
            nxt = front(g + 1)
        xgb = xg.astype(BF16)
        yd = jnp.zeros((q, gw), F32)
        for r in range(hpg):
            h = g * hpg + r
            seg = acs2[:, h:h + 1] - acs2_t[h:h + 1, :]
            dec = jnp.exp2(jnp.where(causal, seg, NEG))
            mh = (cbm * dec * dt_t[h:h + 1, :]).astype(BF16)
            yd = jnp.where((lane_g >= r * hd) & (lane_g < (r + 1) * hd), _dot(mh, xgb), yd)
        yg = yd + yoff * w2g + dsk_ref[:, g * gw:(g + 1) * gw] * xg
        for jj in range(gw // LANES):
            slab = yg[:, jj * LANES:(jj + 1) * LANES]
            cy = g * (gw // LANES) + jj
            if ts == q:
                for ph in range(8):
                    y_ref[0, cy, pl.ds(ph, nph, stride=8), :] = slab[ph * nph:(ph + 1) * nph]
            else:
                for tt in range(ts):
                    pos = (tt % 8) * nph + tt // 8
                    y_ref[0, cy, tt:tt + 1, :] = slab[pos:pos + 1]
        st = _dot((xg * w1g).T.astype(BF16), bb)
        cdt = jnp.concatenate(
            [jnp.broadcast_to(cd[:, g * hpg + r:g * hpg + r + 1], (hd, ds)) for r in range(hpg)], axis=0)
        hst[g * gw:(g + 1) * gw, :] = hg * cdt + st

    for c in range(ncs):
        xpad[c, 8 - pad:8, :] = xpad[c, 8 + ts - pad:8 + ts, :]

    @pl.when(s == pl.num_programs(1) - 1)
    def _():
        hn_ref[nprev, 0] = hst[...]
        if nprev:
            prev_h, prev_c = prev_refs
            hn_ref[0:nprev] = prev_h[...]
            convn_ref[0:nprev] = prev_c[...]


def _ssd(zxbc3, conv0, h0, layer, cw, cb, dtb, alog, dsk, emat, prev, *, nh, hd, ng, ds):
    nb, length, _ = zxbc3.shape
    di = nh * hd
    cdim = di + 2 * ng * ds
    ts = min(CHUNK, length)
    ns = length // ts
    nprev = prev[0].shape[0] if prev else 0
    body = functools.partial(_ssd_body, ts=ts, nh=nh, hd=hd, ng=ng, ds=ds, nprev=nprev)
    assert cdim == 2 * di
    dt_blk = (di + cdim) // LANES
    prev_specs = [
        pl.BlockSpec((nprev, 1, di, ds), lambda b, s: (0, b, 0, 0)),
        pl.BlockSpec((nprev, 1, CONV_W - 1, cdim), lambda b, s: (0, b, 0, 0)),
    ] if prev else []
    return pl.pallas_call(
        body,
        grid=(nb, ns),
        in_specs=[
            pl.BlockSpec((1, ts, di), lambda b, s: (b, s, 1)),
            pl.BlockSpec((1, ts, di), lambda b, s: (b, s, 2)),
            pl.BlockSpec((1, ts, LANES), lambda b, s: (b, s, dt_blk)),
            pl.BlockSpec((1, 1, CONV_W - 1, cdim), lambda b, s: (layer, b, 0, 0)),
            pl.BlockSpec((1, 1, di, ds), lambda b, s: (layer, b, 0, 0)),
            pl.BlockSpec((CONV_W, cdim), lambda b, s: (0, 0)),
            pl.BlockSpec((1, cdim), lambda b, s: (0, 0)),
            pl.BlockSpec((1, LANES), lambda b, s: (0, 0)),
            pl.BlockSpec((1, LANES), lambda b, s: (0, 0)),
            pl.BlockSpec((1, di), lambda b, s: (0, 0)),
            pl.BlockSpec((2 * LANES, di), lambda b, s: (0, 0)),
        ] + prev_specs,
        out_specs=[
            pl.BlockSpec((1, di // LANES, ts, LANES), lambda b, s: (b, 0, s, 0)),
            pl.BlockSpec((nprev + 1, 1, di, ds), lambda b, s: (0, b, 0, 0)),
            pl.BlockSpec((nprev + 1, 1, CONV_W - 1, cdim), lambda b, s: (0, b, 0, 0)),
        ],
        out_shape=[SDS((nb, di // LANES, length, LANES), F32), SDS((nprev + 1, nb, di, ds), F32),
                   SDS((nprev + 1, nb, CONV_W - 1, cdim), F32)],
        scratch_shapes=[
            pltpu.VMEM((cdim // LANES, 8 + CHUNK, LANES), F32),
            pltpu.VMEM((CHUNK, LANES), F32),
            pltpu.VMEM((di, ds), F32),
            pltpu.VMEM((2, CHUNK, 2 * LANES), BF16),
        ],
        compiler_params=_cparams(("parallel", "arbitrary")),
        name="ssd",
    )(zxbc3, zxbc3, zxbc3, conv0, h0, cw, cb, dtb, alog, dsk, emat, *prev)


def _post_body(y_ref, z_ref, x_ref, gw_ref, w_ref, o_ref, *, ng):
    if len(y_ref.shape) == 4:
        y = jnp.concatenate([y_ref[0, c] for c in range(y_ref.shape[1])], axis=1)
    else:
        y = y_ref[...]
    yz = y * _silu(z_ref[...])
    width = yz.shape[1] // ng
    parts = []
    for g in range(ng):
        blk = yz[:, g * width:(g + 1) * width]
        ms = jnp.mean(blk * blk, axis=-1, keepdims=True)
        parts.append(blk * lax.rsqrt(ms + EPS))
    yn = jnp.concatenate(parts, axis=1) * gw_ref[...]
    o_ref[...] = x_ref[...] + _dot(yn.astype(BF16), w_ref[0].astype(BF16))


def _post(y, zxbc2, x2, gw, w_out, layer, tm, ng):
    m, d = x2.shape
    if y.ndim == 4:
        nslab, length = y.shape[1], y.shape[2]
        di = nslab * LANES
        per_seq = length // tm
        y_spec = pl.BlockSpec((1, nslab, tm, LANES), lambda i: (i // per_seq, 0, i % per_seq, 0))
    else:
        di = y.shape[1]
        y_spec = pl.BlockSpec((tm, di), lambda i: (i, 0))
    return pl.pallas_call(
        functools.partial(_post_body, ng=ng),
        grid=(m // tm,),
        in_specs=[
            y_spec,
            pl.BlockSpec((tm, di), lambda i: (i, 0)),
            pl.BlockSpec((tm, d), lambda i: (i, 0)),
            pl.BlockSpec((1, di), lambda i: (0, 0)),
            pl.BlockSpec((1, di, d), lambda i: (layer, 0, 0)),
        ],
        out_specs=pl.BlockSpec((tm, d), lambda i: (i, 0)),
        out_shape=SDS((m, d), F32),
        compiler_params=_cparams(("parallel",)),
        name="post_mamba",
    )(y, zxbc2, x2, gw, w_out)


def _mlp_body(x_ref, g_ref, wu_ref, wd_ref, o_ref, u_scr):
    _mlp_prologue(x_ref, g_ref, u_scr)
    _mlp_epilogue(x_ref, o_ref, _mlp_main(wu_ref, wd_ref, u_scr))


def _mlp_prologue(x_ref, g_ref, u_scr):
    @pl.when(pl.program_id(1) == 0)
    def _():
        u_scr[...] = _rms(x_ref[...], g_ref[...]).astype(BF16)


def _mlp_main(wu_ref, wd_ref, u_scr):
    u = u_scr[...]
    half = wu_ref.shape[2] // 2
    hs = [jnp.maximum(_dot(u, wu_ref[0, :, k * half:(k + 1) * half].astype(BF16)), 0.0) for k in range(2)]
    part = _dot((hs[0] * hs[0]).astype(BF16), wd_ref[0, 0:half, :].astype(BF16))
    return part + _dot((hs[1] * hs[1]).astype(BF16), wd_ref[0, half:2 * half, :].astype(BF16))


def _mlp_epilogue(x_ref, o_ref, part):
    c = pl.program_id(1)

    @pl.when(c == 0)
    def _():
        o_ref[...] = x_ref[...] + part

    @pl.when(c != 0)
    def _():
        o_ref[...] += part


def _mlp(x2, g, w_up, w_down, layer, tm, tf):
    m, d = x2.shape
    ff = w_up.shape[2]
    return pl.pallas_call(
        _mlp_body,
        grid=(m // tm, ff // tf),
        in_specs=[
            pl.BlockSpec((tm, d), lambda i, c: (i, 0)),
            pl.BlockSpec((1, d), lambda i, c: (0, 0)),
            pl.BlockSpec((1, d, tf), lambda i, c: (layer, 0, c)),
            pl.BlockSpec((1, tf, d), lambda i, c: (layer, c, 0)),
        ],
        out_specs=pl.BlockSpec((tm, d), lambda i, c: (i, 0)),
        out_shape=SDS((m, d), F32),
        scratch_shapes=[pltpu.VMEM((tm, d), BF16)],
        compiler_params=pltpu.CompilerParams(dimension_semantics=("parallel", "arbitrary"),
                                             vmem_limit_bytes=MLP_VMEM_LIMIT),
        name="mlp",
    )(x2, g, w_up, w_down)


def _row_group_norm(xt, width):
    n, tm = xt.shape
    x3 = xt.reshape(n // width, width, tm)
    ms = jnp.mean(x3 * x3, axis=1, keepdims=True)
    return (x3 * lax.rsqrt(ms + EPS)).reshape(n, tm)


def _kv_body(x_ref, g_ref, wkt_ref, wv_ref, kn_ref, kt_ref, v_ref, *head_refs, nha, ahd):
    u = _rms(x_ref[0], g_ref[...]).astype(BF16)
    kt = _row_group_norm(_dot_nt(wkt_ref[...], u), ahd) * kn_ref[...]
    v = _dot(u, wv_ref[...])
    kt_ref[0] = kt
    v_ref[0] = v
    if head_refs:
        kh_ref, vht_ref = head_refs
        k = kt.T
        for h in range(nha):
            kh_ref[0, h] = k[:, h * LANES:(h + 1) * LANES].astype(BF16)
            vht_ref[0, h] = v[:, h * LANES:(h + 1) * LANES].T.astype(BF16)


def _kv(x3, g, wk_t, wv, kn_col, tm, nha, ahd, head_major):
    nb, t, d = x3.shape
    kd = nha * LANES
    out_specs = [
        pl.BlockSpec((1, kd, tm), lambda b, i: (b, 0, i)),
        pl.BlockSpec((1, tm, kd), lambda b, i: (b, i, 0)),
    ]
    out_shape = [SDS((nb, kd, t), F32), SDS((nb, t, kd), F32)]
    if head_major:
        out_specs += [
            pl.BlockSpec((1, nha, tm, LANES), lambda b, i: (b, 0, i, 0)),
            pl.BlockSpec((1, nha, LANES, tm), lambda b, i: (b, 0, 0, i)),
        ]
        out_shape += [SDS((nb, nha, t, LANES), BF16), SDS((nb, nha, LANES, t), BF16)]
    return pl.pallas_call(
        functools.partial(_kv_body, nha=nha, ahd=ahd),
        grid=(nb, t // tm),
        in_specs=[
            pl.BlockSpec((1, tm, d), lambda b, i: (b, i, 0)),
            pl.BlockSpec((1, d), lambda b, i: (0, 0)),
            pl.BlockSpec((kd, d), lambda b, i: (0, 0)),
            pl.BlockSpec((d, kd), lambda b, i: (0, 0)),
            pl.BlockSpec((kd, 1), lambda b, i: (0, 0)),
        ],
        out_specs=out_specs,
        out_shape=out_shape,
        compiler_params=_cparams(("parallel", "parallel")),
        name="kv_proj",
    )(x3, g, wk_t, wv, kn_col)


def _q_body(x_ref, g_ref, w_ref, qn_ref, gm_ref, q_ref, *, scale):
    u = _rms(x_ref[0], g_ref[...]).astype(BF16)
    qn = _group_norm64(_dot(u, w_ref[0].astype(BF16)), gm_ref[...]) * qn_ref[...] * scale
    q_ref[0] = qn.astype(BF16)


def _qproj(x3, g, w_q, layer, qn_t, gmat, tm, nha, scale):
    nb, t, d = x3.shape
    kd = nha * LANES
    return pl.pallas_call(
        functools.partial(_q_body, scale=scale),
        grid=(nb, t // tm),
        in_specs=[
            pl.BlockSpec((1, tm, d), lambda b, i: (b, i, 0)),
            pl.BlockSpec((1, d), lambda b, i: (0, 0)),
            pl.BlockSpec((1, d, kd), lambda b, i: (layer, 0, 0)),
            pl.BlockSpec((1, kd), lambda b, i: (0, 0)),
            pl.BlockSpec((256, 256), lambda b, i: (0, 0)),
        ],
        out_specs=pl.BlockSpec((1, tm, kd), lambda b, i: (b, i, 0)),
        out_shape=SDS((nb, t, kd), BF16),
        compiler_params=_cparams(("parallel", "parallel")),
        name="q_proj",
    )(x3, g, w_q, qn_t, gmat)


def _lam(l1, k1, l2, k2, lam_init):
    return (jnp.exp(jnp.sum(l1 * k1, axis=-1, keepdims=True))
            - jnp.exp(jnp.sum(l2 * k2, axis=-1, keepdims=True)) + lam_init)


def _attn_body(it_ref, jt_ref, x_ref, g_ref, wqt_ref, qn_ref, k_ref, vt_ref,
               l1_ref, k1_ref, l2_ref, k2_ref, sw_ref, wo_ref, y_ref,
               qt_ref, m_scr, l_scr, acc_scr, st_scr, mx_scr, *, tq, tk, nha, ahd, qscale, lam_init):
    t = pl.program_id(1)
    i = it_ref[t]
    j = jt_ref[t]

    @pl.when(j == 0)
    def _():
        m_scr[...] = jnp.full_like(m_scr, NEG)
        l_scr[...] = jnp.zeros_like(l_scr)
        acc_scr[...] = jnp.zeros_like(acc_scr)
        u = _rms(x_ref[0], g_ref[...]).astype(BF16)
        qt = _row_group_norm(_dot_nt(wqt_ref[0], u), ahd) * (qn_ref[...] * qscale)
        rows = lax.broadcasted_iota(jnp.int32, (LANES, tq), 0)
        for h in range(nha):
            blk = qt[h * LANES:(h + 1) * LANES, :]
            qt_ref[h, 0] = jnp.where(rows < ahd, blk, 0.0).astype(BF16)
            qt_ref[h, 1] = jnp.where(rows >= ahd, blk, 0.0).astype(BF16)

    def scores(h, c, slot, bias):
        st = _dot(k_ref[0, h], qt_ref[h, c])
        if bias is not None:
            st = st + bias
        st_scr[slot] = st
        mx_scr[slot] = jnp.max(st, axis=0, keepdims=True)

    def accumulate(h, c, slot):
        m_prev = m_scr[h, c]
        m_new = jnp.maximum(m_prev, mx_scr[slot])
        alpha = jnp.exp2(m_prev - m_new)
        p = jnp.exp2(st_scr[slot] - m_new)
        l_scr[h, c] = alpha * l_scr[h, c] + jnp.sum(p, axis=0, keepdims=True)
        acc_scr[h, c] = alpha * acc_scr[h, c] + _dot(vt_ref[0, h], p.astype(BF16))
        m_scr[h, c] = m_new

    def run_heads(bias):
        scores(0, 0, 0, bias)

        def head(h, carry):
            scores(h, 1, 1, bias)
            accumulate(h, 0, 0)
            scores(jnp.minimum(h + 1, nha - 1), 0, 0, bias)
            accumulate(h, 1, 1)
            return carry

        lax.fori_loop(0, nha, head, 0, unroll=4)

    needs_mask = (j + 1) * tk - 1 > i * tq

    @pl.when(needs_mask)
    def _():
        kpos = j * tk + lax.broadcasted_iota(jnp.int32, (tk, tq), 0)
        qpos = i * tq + lax.broadcasted_iota(jnp.int32, (tk, tq), 1)
        run_heads(jnp.where(kpos <= qpos, 0.0, NEG))

    @pl.when(jnp.logical_not(needs_mask))
    def _():
        run_heads(None)

    @pl.when((j + 1) * tk >= (i + 1) * tq)
    def _():
        lam = _lam(l1_ref[...], k1_ref[...], l2_ref[...], k2_ref[...], lam_init)
        parts = []
        for h in range(nha):
            ot = (acc_scr[h, 0] * (1.0 / l_scr[h, 0])
                  - (lam * (1.0 / l_scr[h, 1])) * acc_scr[h, 1])
            ms = jnp.mean(ot * ot, axis=0, keepdims=True)
            parts.append((ot * lax.rsqrt(ms + EPS)).T)
        on = jnp.concatenate(parts, axis=1) * (sw_ref[...] * (1.0 - lam_init))
        y_ref[0] = x_ref[0] + _dot(on.astype(BF16), wo_ref[0].astype(BF16))


def _attn_prompt(x3, g, wq_t, qn_col, kh, vht, l1, k1, l2, k2, sw_t, w_o, layer, tq, tk, ahd, qscale,
                 lam_init):
    nb, t, d = x3.shape
    nha = kh.shape[1]
    kd = nha * LANES
    hd = ahd
    its, jts = [], []
    for i in range(t // tq):
        for j in range(((i + 1) * tq + tk - 1) // tk):
            its.append(i)
            jts.append(j)
    it = jnp.asarray(its, jnp.int32)
    jt = jnp.asarray(jts, jnp.int32)
    lspec = pl.BlockSpec((1, hd), lambda b, s, it, jt: (0, 0))
    grid_spec = pltpu.PrefetchScalarGridSpec(
        num_scalar_prefetch=2,
        grid=(nb, len(its)),
        in_specs=[
            pl.BlockSpec((1, tq, d), lambda b, s, it, jt: (b, it[s], 0)),
            pl.BlockSpec((1, d), lambda b, s, it, jt: (0, 0)),
            pl.BlockSpec((1, kd, d), lambda b, s, it, jt: (layer, 0, 0)),
            pl.BlockSpec((kd, 1), lambda b, s, it, jt: (0, 0)),
            pl.BlockSpec((1, nha, tk, LANES), lambda b, s, it, jt: (b, 0, jt[s], 0)),
            pl.BlockSpec((1, nha, LANES, tk), lambda b, s, it, jt: (b, 0, 0, jt[s])),
            lspec, lspec, lspec, lspec,
            pl.BlockSpec((1, kd), lambda b, s, it, jt: (0, 0)),
            pl.BlockSpec((1, kd, d), lambda b, s, it, jt: (layer, 0, 0)),
        ],
        out_specs=pl.BlockSpec((1, tq, d), lambda b, s, it, jt: (b, it[s], 0)),
        scratch_shapes=[
            pltpu.VMEM((nha, 2, LANES, tq), BF16),
            pltpu.VMEM((nha, 2, 1, tq), F32),
            pltpu.VMEM((nha, 2, 1, tq), F32),
            pltpu.VMEM((nha, 2, LANES, tq), F32),
            pltpu.VMEM((2, tk, tq), F32),
            pltpu.VMEM((2, 1, tq), F32),
        ],
    )
    return pl.pallas_call(
        functools.partial(_attn_body, tq=tq, tk=tk, nha=nha, ahd=ahd, qscale=qscale, lam_init=lam_init),
        grid_spec=grid_spec,
        out_shape=SDS((nb, t, d), F32),
        compiler_params=_cparams(("parallel", "arbitrary")),
        name="attn_prompt",
    )(it, jt, x3, g, wq_t, qn_col, kh, vht, l1, k1, l2, k2, sw_t, w_o)


def _decode_body(pt_ref, q_ref, kn_ref, vn_ref, l1_ref, k1_ref, l2_ref, k2_ref, *rest,
                 pps, tn, nha, hd, lam_init):
    _decode_step(pl.program_id(0), pl.program_id(1), pl.num_programs(1), q_ref, kn_ref, vn_ref,
                 (l1_ref, k1_ref, l2_ref, k2_ref), rest[:pps], rest[pps:2 * pps], rest[2 * pps],
                 rest[2 * pps + 1:], tn=tn, nha=nha, hd=hd, lam_init=lam_init)


def _decode_step(b_idx, j, nj, q_ref, kn_ref, vn_ref, lam_refs, k_refs, v_refs, o_ref, scratch,
                 *, tn, nha, hd, lam_init, phases=("setup", "pages", "finish")):
    l1_ref, k1_ref, l2_ref, k2_ref = lam_refs
    qb, m_scr, l_scr, acc_scr = scratch
    pps = len(k_refs)
    nmap = 2 * nha
    nrow = tn * nmap
    kd = nha * LANES
    page = k_refs[0].shape[2]

    def online(sc, vals):
        n = len(vals)
        m_prev = m_scr[...]
        blk_max = sc[:, 0:LANES]
        for b in range(1, n):
            blk_max = jnp.maximum(blk_max, sc[:, b * LANES:(b + 1) * LANES])
        m_new = jnp.maximum(m_prev, jnp.max(blk_max, axis=-1, keepdims=True))
        alpha = jnp.exp(m_prev - m_new)
        p = jnp.exp(sc - jnp.concatenate([m_new] * n, axis=1))
        psum = p[:, 0:LANES]
        for b in range(1, n):
            psum = psum + p[:, b * LANES:(b + 1) * LANES]
        l_scr[...] = alpha * l_scr[...] + psum
        pv = None
        for b in range(0, n, 2):
            nb2 = min(2, n - b)
            vblk = vals[b] if nb2 == 1 else jnp.concatenate(vals[b:b + 2], axis=0)
            term = _dot(p[:, b * LANES:(b + nb2) * LANES].astype(BF16), vblk)
            pv = term if pv is None else pv + term
        acc_scr[...] = jnp.concatenate([alpha] * (kd // LANES), axis=1) * acc_scr[...] + pv
        m_scr[...] = m_new

    def page_values(v_ref):
        return jnp.concatenate(
            [v_ref[0, pl.ds(h, page, stride=nha), :] for h in range(nha)], axis=1).astype(BF16)

    def when(phase, cond):
        return pl.when(cond) if phase in phases else (lambda fn: None)

    @when("setup", j == 0)
    def _():
        rid = lax.broadcasted_iota(jnp.int32, (nmap, kd), 0)
        lid = lax.broadcasted_iota(jnp.int32, (nmap, kd), 1)
        sel = (lid >= rid * hd) & (lid < (rid + 1) * hd)
        qf = q_ref[0].astype(F32)
        for tt in range(tn):
            rowq = jnp.broadcast_to(qf[tt:tt + 1, :], (nmap, kd))
            qb[tt * nmap:(tt + 1) * nmap, :] = jnp.where(sel, rowq, 0.0).astype(BF16)
        m_scr[...] = jnp.full_like(m_scr, NEG)
        l_scr[...] = jnp.zeros_like(l_scr)
        acc_scr[...] = jnp.zeros_like(acc_scr)
        nnew = kn_ref.shape[2]
        qrow = lax.broadcasted_iota(jnp.int32, (nrow, LANES), 0)
        ktok = lax.broadcasted_iota(jnp.int32, (nrow, LANES), 1) - b_idx * tn
        for blk in range(nnew // LANES):
            sc = _dot(qb[...], kn_ref[0, :, blk * LANES:(blk + 1) * LANES].astype(BF16))
            kt = ktok - blk * LANES
            sc = jnp.where((kt >= 0) & (kt * nmap <= qrow), sc, NEG)
            online(sc, [vn_ref[0, blk * LANES:(blk + 1) * LANES, :].astype(BF16)])

    if "pages" in phases:
        scs = [_dot(qb[...], jnp.concatenate([k_refs[b + i][0].astype(BF16) for i in range(min(2, pps - b))],
                                             axis=1)) for b in range(0, pps, 2)]
        online(jnp.concatenate(scs, axis=1), [page_values(v_refs[b]) for b in range(pps)])

    @when("finish", j == nj - 1)
    def _():
        lam = _lam(l1_ref[...], k1_ref[...], l2_ref[...], k2_ref[...], lam_init)
        inv_l = 1.0 / jnp.sum(l_scr[...], axis=-1, keepdims=True)
        rid = lax.broadcasted_iota(jnp.int32, (nmap, 1), 0)
        rid2 = lax.broadcasted_iota(jnp.int32, (nmap, kd), 0)
        lid2 = lax.broadcasted_iota(jnp.int32, (nmap, kd), 1)
        own = (lid2 >= (rid2 >> 1) * LANES) & (lid2 < ((rid2 >> 1) + 1) * LANES)
        for tt in range(tn):
            il = inv_l[tt * nmap:(tt + 1) * nmap, :]
            coef = jnp.where((rid & 1) == 0, il, -lam * il)
            w = jnp.where(own, acc_scr[tt * nmap:(tt + 1) * nmap, :] * coef, 0.0)
            o_ref[0, tt:tt + 1, :] = jnp.sum(w, axis=0, keepdims=True)


def _attn_sample(page_table, q3, kt_new, v_new, cache_kt, cache_v2, l1, k1, l2, k2, pps, hd, lam_init):
    nb, tn, kd = q3.shape
    nha = kd // LANES
    n_pages = page_table.shape[1]
    page = cache_kt.shape[2]
    nnew = kt_new.shape[2]
    nrow = tn * 2 * nha
    assert page == LANES and nnew == LANES and nb * tn == nnew
    pt_flat = page_table.reshape(-1)

    def k_spec(b_off):
        return pl.BlockSpec((1, kd, page),
                            lambda b, j, pt: (pt[b * n_pages + j * pps + b_off], 0, 0))

    def v_spec(b_off):
        return pl.BlockSpec((1, page * nha, LANES),
                            lambda b, j, pt: (pt[b * n_pages + j * pps + b_off], 0, 0))

    lspec = pl.BlockSpec((1, hd), lambda b, j, pt: (0, 0))
    grid_spec = pltpu.PrefetchScalarGridSpec(
        num_scalar_prefetch=1,
        grid=(nb, n_pages // pps),
        in_specs=[
            pl.BlockSpec((1, tn, kd), lambda b, j, pt: (b, 0, 0)),
            pl.BlockSpec((1, kd, nnew), lambda b, j, pt: (0, 0, 0)),
            pl.BlockSpec((1, nnew, kd), lambda b, j, pt: (0, 0, 0)),
            lspec, lspec, lspec, lspec]
        + [k_spec(b) for b in range(pps)] + [v_spec(b) for b in range(pps)],
        out_specs=pl.BlockSpec((1, tn, kd), lambda b, j, pt: (b, 0, 0)),
        scratch_shapes=[
            pltpu.VMEM((nrow, kd), BF16),
            pltpu.VMEM((nrow, LANES), F32),
            pltpu.VMEM((nrow, LANES), F32),
            pltpu.VMEM((nrow, kd), F32),
        ],
    )
    return pl.pallas_call(
        functools.partial(_decode_body, pps=pps, tn=tn, nha=nha, hd=hd, lam_init=lam_init),
        grid_spec=grid_spec,
        out_shape=SDS((nb, tn, kd), F32),
        compiler_params=_cparams(("parallel", "arbitrary")),
        name="attn_sample",
    )(pt_flat, q3, kt_new, v_new, l1, k1, l2, k2, *([cache_kt] * pps), *([cache_v2] * pps))


def _mlp_decode_body(pt_ref, x_ref, g_ref, wu_ref, wd_ref, q_ref, kn_ref, vn_ref,
                     l1_ref, k1_ref, l2_ref, k2_ref, *rest, pps, spp, seq0, tn, nha, hd, lam_init):
    k_refs = rest[:pps]
    v_refs = rest[pps:2 * pps]
    o_ref, a_ref = rest[2 * pps], rest[2 * pps + 1]
    u_scr = rest[2 * pps + 2]
    step = pl.program_id(0) * pl.num_programs(1) + pl.program_id(1)
    decode = functools.partial(
        _decode_step, seq0 + step // spp, step % spp, spp, q_ref, kn_ref, vn_ref,
        (l1_ref, k1_ref, l2_ref, k2_ref), k_refs, v_refs, a_ref, rest[2 * pps + 3:],
        tn=tn, nha=nha, hd=hd, lam_init=lam_init)
    _mlp_prologue(x_ref, g_ref, u_scr)
    decode(phases=("setup",))
    part = _mlp_main(wu_ref, wd_ref, u_scr)
    decode(phases=("pages",))
    _mlp_epilogue(x_ref, o_ref, part)
    decode(phases=("finish",))


def _mlp_decode(x2, g, w_up, w_down, layer, tm, tf, page_table, q3, kt_new, v_new, cache_kt, cache_v2,
                l1, k1, l2, k2, seq0, nseq, pps, hd, lam_init):
    m, d = x2.shape
    ff = w_up.shape[2]
    _, tn, kd = q3.shape
    nha = kd // LANES
    n_pages = page_table.shape[1]
    page = cache_kt.shape[2]
    nnew = kt_new.shape[2]
    nrow = tn * 2 * nha
    spp = n_pages // pps
    nc = ff // tf
    assert (m // tm) * nc == nseq * spp and page == LANES and nnew == LANES
    pt_flat = page_table.reshape(-1)

    def seq_of(i, c):
        return seq0 + (i * nc + c) // spp

    def page_of(i, c, pt, b_off):
        return pt[seq_of(i, c) * n_pages + ((i * nc + c) % spp) * pps + b_off]

    def k_spec(b_off):
        return pl.BlockSpec((1, kd, page), lambda i, c, pt: (page_of(i, c, pt, b_off), 0, 0))

    def v_spec(b_off):
        return pl.BlockSpec((1, page * nha, LANES), lambda i, c, pt: (page_of(i, c, pt, b_off), 0, 0))

    lspec = pl.BlockSpec((1, hd), lambda i, c, pt: (0, 0))
    grid_spec = pltpu.PrefetchScalarGridSpec(
        num_scalar_prefetch=1,
        grid=(m // tm, nc),
        in_specs=[
            pl.BlockSpec((tm, d), lambda i, c, pt: (i, 0)),
            pl.BlockSpec((1, d), lambda i, c, pt: (0, 0)),
            pl.BlockSpec((1, d, tf), lambda i, c, pt: (layer, 0, c)),
            pl.BlockSpec((1, tf, d), lambda i, c, pt: (layer, c, 0)),
            pl.BlockSpec((1, tn, kd), lambda i, c, pt: (seq_of(i, c), 0, 0)),
            pl.BlockSpec((1, kd, nnew), lambda i, c, pt: (0, 0, 0)),
            pl.BlockSpec((1, nnew, kd), lambda i, c, pt: (0, 0, 0)),
            lspec, lspec, lspec, lspec]
        + [k_spec(b) for b in range(pps)] + [v_spec(b) for b in range(pps)],
        out_specs=[
            pl.BlockSpec((tm, d), lambda i, c, pt: (i, 0)),
            pl.BlockSpec((1, tn, kd), lambda i, c, pt: (seq_of(i, c) - seq0, 0, 0)),
        ],
        scratch_shapes=[
            pltpu.VMEM((tm, d), BF16),
            pltpu.VMEM((nrow, kd), BF16),
            pltpu.VMEM((nrow, LANES), F32),
            pltpu.VMEM((nrow, LANES), F32),
            pltpu.VMEM((nrow, kd), F32),
        ],
    )
    return pl.pallas_call(
        functools.partial(_mlp_decode_body, pps=pps, spp=spp, seq0=seq0, tn=tn, nha=nha, hd=hd,
                          lam_init=lam_init),
        grid_spec=grid_spec,
        out_shape=[SDS((m, d), F32), SDS((nseq, tn, kd), F32)],
        compiler_params=pltpu.CompilerParams(dimension_semantics=("arbitrary", "arbitrary"),
                                             vmem_limit_bytes=MLP_VMEM_LIMIT),
        name="mlp_decode",
    )(pt_flat, x2, g, w_up, w_down, q3, kt_new, v_new, l1, k1, l2, k2,
      *([cache_kt] * pps), *([cache_v2] * pps))


def _attn_out_body(o_ref, x_ref, sw_ref, w_ref, y_ref, *, nha, post_scale):
    o = o_ref[...]
    parts = []
    for h in range(nha):
        blk = o[:, h * LANES:(h + 1) * LANES]
        ms = jnp.mean(blk * blk, axis=-1, keepdims=True)
        parts.append(blk * lax.rsqrt(ms + EPS))
    on = jnp.concatenate(parts, axis=1) * sw_ref[...] * post_scale
    y_ref[...] = x_ref[...] + _dot(on.astype(BF16), w_ref[0].astype(BF16))


def _attn_out(o2, x2, sw_t, w_o, layer, tm, nha, post_scale):
    m, d = x2.shape
    vd = o2.shape[1]
    return pl.pallas_call(
        functools.partial(_attn_out_body, nha=nha, post_scale=post_scale),
        grid=(m // tm,),
        in_specs=[
            pl.BlockSpec((tm, vd), lambda i: (i, 0)),
            pl.BlockSpec((tm, d), lambda i: (i, 0)),
            pl.BlockSpec((1, vd), lambda i: (0, 0)),
            pl.BlockSpec((1, vd, d), lambda i: (layer, 0, 0)),
        ],
        out_specs=pl.BlockSpec((tm, d), lambda i: (i, 0)),
        out_shape=SDS((m, d), F32),
        compiler_params=_cparams(("parallel",)),
        name="attn_out",
    )(o2, x2, sw_t, w_o)


def kernel(x_prompt, x_sample, state_ssm, state_conv, cache_k, cache_v, page_table, norm_a, w_in_a, conv_w, conv_b, dt_bias, a_log, d_skip, gnorm_w, w_out_a, kv_norm, w_kv, k_norm, norm_b, w_q, q_norm, lam_q1, lam_k1, lam_q2, lam_k2, subln, w_o, norm_m, w_up, w_down):
    n_a = norm_a.shape[0]
    n_b = norm_b.shape[0]
    d = x_prompt.shape[-1]
    nh, hd, ds = state_ssm.shape[2], state_ssm.shape[3], state_ssm.shape[4]
    di = nh * hd
    cdim = state_conv.shape[-1]
    ng = (cdim - di) // (2 * ds)
    nha, ahd = cache_k.shape[2], cache_k.shape[4]
    kd = nha * 2 * ahd
    page = cache_k.shape[1]
    assert 2 * ahd == LANES and cache_v.shape[3] == LANES and nh <= LANES

    emat = (jnp.arange(LANES)[:, None] == (jnp.arange(di)[None, :] // hd)).astype(BF16)
    emat = jnp.concatenate([emat, emat], axis=0)
    gidx = jnp.arange(256) // ahd
    gmat = jnp.where(gidx[:, None] == gidx[None, :], 1.0 / ahd, 0.0).astype(BF16)

    def row(v):
        return v.reshape(1, -1).astype(F32)

    def pad_lanes(v):
        return jnp.pad(v.astype(F32), (0, LANES - v.shape[0])).reshape(1, LANES)

    n_in = di + cdim + nh
    tn_in = -(-n_in // (2 * LANES)) * LANES
    w_in_b = jnp.pad(w_in_a, ((0, 0), (0, 0), (0, 2 * tn_in - n_in))).astype(BF16)
    wk_t_b = w_kv[:, :kd].T.astype(BF16)
    wv_b = w_kv[:, kd:].astype(BF16)
    wq_t_b = jnp.swapaxes(w_q, 1, 2).astype(BF16)
    scale = ahd ** -0.5

    def trunk(x3, conv0, ssm0, seq_major):
        nb, t, _ = x3.shape
        m = nb * t
        tm = min(512, m)
        x2 = x3.reshape(m, d)
        ab, at = (nb, t) if seq_major else (1, m)
        tma = min(512, at)
        states = ()
        k_new = v_new = kv_ctx = None
        for l in range(n_a + n_b):
            if l < n_a:
                zxbc = _inproj(x2, row(norm_a[l]), w_in_b, l, tm, tn_in)
                y3, hn, cn = _ssd(
                    zxbc.reshape(nb, t, 2 * tn_in), conv0, ssm0.reshape(n_a, nb, di, ds), l,
                    conv_w[l], row(conv_b[l]), pad_lanes(dt_bias[l]), pad_lanes(a_log[l]),
                    row(jnp.repeat(d_skip[l], hd)), emat, states, nh=nh, hd=hd, ng=ng, ds=ds)
                states = (hn, cn)
                tmp = min(256, m)
                if t % tmp:
                    y3 = jnp.transpose(y3, (0, 2, 1, 3)).reshape(m, di)
                x2 = _post(y3, zxbc, x2, row(gnorm_w[l]), w_out_a, l, tmp, ng)
            else:
                jb = l - n_a
                if l == n_a:
                    kv_ctx = _kv(x2.reshape(ab, at, d), row(kv_norm), wk_t_b, wv_b,
                                 jnp.tile(k_norm, 2 * nha).reshape(kd, 1).astype(F32), tma, nha, ahd, seq_major)
                    k_new = jnp.transpose(kv_ctx[0].reshape(ab, nha, 2, ahd, at), (0, 4, 1, 2, 3))
                    k_new = k_new.reshape(nb, t, nha, 2, ahd)
                    v_new = kv_ctx[1].reshape(nb, t, nha, LANES)
                lam_init = 0.8 - 0.6 * math.exp(-0.3 * l)
                lams = (row(lam_q1[jb]), row(lam_k1[jb]), row(lam_q2[jb]), row(lam_k2[jb]))
                qn_t = jnp.tile(q_norm[jb], 2 * nha).astype(F32)
                sw_t = row(jnp.tile(subln[jb], nha))
                if seq_major:
                    x2 = _attn_prompt(x2.reshape(nb, t, d), row(norm_b[jb]), wq_t_b, qn_t.reshape(kd, 1),
                                      kv_ctx[2], kv_ctx[3], *lams, sw_t, w_o, jb, 512, 512, ahd,
                                      scale * LOG2E, lam_init).reshape(m, d)
                else:
                    q3 = _qproj(x2.reshape(ab, at, d), row(norm_b[jb]), w_q, jb,
                                qn_t.reshape(1, kd), gmat, tma, nha, scale)
                    o3 = yield ("attend", (q3.reshape(nb, t, kd), kv_ctx[0], kv_ctx[1], lams, lam_init))
                    x2 = _attn_out(o3.reshape(m, kd), x2, sw_t, w_o, jb, tm, nha, 1.0 - lam_init)
            if seq_major:
                x2 = yield ("mlp", (x2, row(norm_m[l]), l))
            else:
                x2 = _mlp(x2, row(norm_m[l]), w_up_b, w_down_b, l, min(1024, m), 1024)
        return x2.reshape(nb, t, d), states[0].reshape(n_a, nb, nh, hd, ds), states[1], k_new, v_new

    n_pool = cache_k.shape[0]
    cache_kt = jnp.transpose(cache_k, (0, 2, 3, 4, 1)).reshape(n_pool, kd, page)
    cache_v2 = cache_v.reshape(n_pool, page * nha, LANES)

    bp = x_prompt.shape[0]
    conv0_p = jnp.zeros((n_a, bp, CONV_W - 1, cdim), F32)
    ssm0_p = jnp.zeros((n_a, bp, nh, hd, ds), F32)
    prompt = trunk(x_prompt, conv0_p, ssm0_p, True)
    sample = trunk(x_sample, state_conv, state_ssm, False)

    m_p = bp * x_prompt.shape[1]
    nb_s, n_pages = page_table.shape
    share = (n_a + n_b) // n_b
    tm_f, tf_f, pps_f = 1024, 512, 8
    w_up_b, w_down_b = w_up.astype(BF16), w_down.astype(BF16)
    fused = ((n_a + n_b) % n_b == 0 and nb_s % share == 0 and m_p % tm_f == 0 and n_pages % pps_f == 0
             and (m_p // tm_f) * (w_up.shape[2] // tf_f) == (nb_s // share) * (n_pages // pps_f))

    def advance(gen, value):
        try:
            return gen.send(value), None
        except StopIteration as stop:
            return None, stop.value

    req_s, out_s = advance(sample, None)
    req_p, out_p = advance(prompt, None)
    while req_p is not None or req_s is not None:
        if fused and req_p is not None and req_s is not None:
            q3, kt_new, v_new, lams, lam_init = req_s[1]
            nseq = nb_s // share
            parts = []
            for k in range(share):
                x2, g_m, l = req_p[1]
                x2, o_part = _mlp_decode(x2, g_m, w_up_b, w_down_b, l, tm_f, tf_f, page_table, q3, kt_new, v_new,
                                         cache_kt, cache_v2, *lams, k * nseq, nseq, pps_f, ahd, lam_init)
                parts.append(o_part)
                req_p, out_p = advance(prompt, x2)
            req_s, out_s = advance(sample, jnp.concatenate(parts, axis=0))
        elif req_p is not None:
            x2, g_m, l = req_p[1]
            req_p, out_p = advance(prompt, _mlp(x2, g_m, w_up, w_down, l, min(1024, m_p), 1024))
        else:
            q3, kt_new, v_new, lams, lam_init = req_s[1]
            req_s, out_s = advance(sample, _attn_sample(page_table, q3, kt_new, v_new, cache_kt, cache_v2,
                                                        *lams, 16, ahd, lam_init))
    y_p, ssm_p, conv_p, k_p, v_p = out_p
    y_s, ssm_s, conv_s, k_s, v_s = out_s
    return (y_p, y_s, ssm_p, conv_p, k_p, v_p, ssm_s, conv_s, k_s, v_s)
```

```python
import functools
import math

import jax
import jax.numpy as jnp
from jax import lax
from jax.experimental import pallas as pl
from jax.experimental.pallas import tpu as pltpu

F32 = jnp.float32
BF16 = jnp.bfloat16
EPS = 1e-6
LANES = 128
CHUNK = 128
CONV_W = 4
VMEM_LIMIT = 48 * 1024 * 1024
MLP_VMEM_LIMIT = 56 * 1024 * 1024
NEG = -1e30
LOG2E = math.log2(math.e)
SDS = jax.ShapeDtypeStruct


def _cparams(sem):
    return pltpu.CompilerParams(dimension_semantics=sem, vmem_limit_bytes=VMEM_LIMIT)


def _rms(x, g):
    ms = jnp.mean(x * x, axis=-1, keepdims=True)
    return x * lax.rsqrt(ms + EPS) * g


def _silu(x):
    return x * (1.0 / (1.0 + jnp.exp2(x * -LOG2E)))


def _softplus(x):
    e = jnp.exp(-jnp.abs(x))
    u = 1.0 + e
    log1p_e = jnp.where(u == 1.0, e, jnp.log(u) * (e / (u - 1.0)))
    return jnp.maximum(x, 0.0) + log1p_e


def _dot(a, b):
    return jnp.dot(a, b, preferred_element_type=F32)


def _dot_nt(a, b):
    return lax.dot_general(a, b, (((1,), (1,)), ((), ())), preferred_element_type=F32)


def _split_bf16(x):
    hi = x.astype(BF16)
    lo = (x - hi.astype(F32)).astype(BF16)
    return hi, lo


def _group_norm64(x, gmat):
    parts = []
    for c in range(x.shape[1] // 256):
        blk = x[:, c * 256:(c + 1) * 256]
        ms = _dot((blk * blk).astype(BF16), gmat)
        parts.append(blk * lax.rsqrt(ms + EPS))
    return jnp.concatenate(parts, axis=1)


def _inproj_body(x_ref, g_ref, w_ref, o_ref):
    u = _rms(x_ref[...], g_ref[...]).astype(BF16)
    o_ref[...] = _dot(u, w_ref[0])


def _inproj(x2, g, w_all, layer, tm, tn):
    m, d = x2.shape
    n = w_all.shape[2]
    return pl.pallas_call(
        _inproj_body,
        grid=(n // tn, m // tm),
        in_specs=[
            pl.BlockSpec((tm, d), lambda j, i: (i, 0)),
            pl.BlockSpec((1, d), lambda j, i: (0, 0)),
            pl.BlockSpec((1, d, tn), lambda j, i: (layer, 0, j)),
        ],
        out_specs=pl.BlockSpec((tm, tn), lambda j, i: (i, j)),
        out_shape=SDS((m, n), F32),
        compiler_params=_cparams(("parallel", "parallel")),
        name="inproj",
    )(x2, g, w_all)


def _ssd_body(xa_ref, xb_ref, dt_ref, conv0_ref, h0_ref, cw_ref, cb_ref, dtb_ref, alog_ref,
              dsk_ref, e_ref, *rest, ts, nh, hd, ng, ds, nprev):
    prev_refs, (y_ref, hn_ref, convn_ref, xpad, dtpad, hst, wsp) = rest[:len(rest) - 7], rest[-7:]
    q = CHUNK
    nph = q // 8
    di = nh * hd
    gn = ng * ds
    hpg = nh // ng
    gw = hpg * hd
    nxs = di // LANES
    ncs = xpad.shape[0]
    pad = CONV_W - 1
    s = pl.program_id(1)

    @pl.when(s == 0)
    def _():
        xpad[...] = jnp.zeros_like(xpad)
        dtpad[...] = jnp.zeros_like(dtpad)
        for c in range(ncs):
            xpad[c, 8 - pad:8, :] = conv0_ref[0, 0, :, c * LANES:(c + 1) * LANES]
        hst[...] = h0_ref[0, 0]

    for c in range(ncs):
        src = xa_ref if c < nxs else xb_ref
        cc0 = (c % nxs) * LANES
        xpad[c, 8:8 + ts, :] = src[0, :, cc0:cc0 + LANES]
        convn_ref[nprev, 0, :, c * LANES:(c + 1) * LANES] = xpad[c, 8 + ts - pad:8 + ts, :]
    dtpad[0:ts, :] = dt_ref[0]

    def phase_rows(ref2d, r0):
        return jnp.concatenate([ref2d[pl.ds(r0 + ph, nph, stride=8), :] for ph in range(8)], axis=0)

    def conv_silu(c0, width):
        slabs = []
        for c in range(c0 // LANES, (c0 + width) // LANES):
            acc = cb_ref[:, c * LANES:(c + 1) * LANES]
            for k in range(CONV_W):
                acc = acc + cw_ref[k:k + 1, c * LANES:(c + 1) * LANES] * phase_rows(xpad.at[c], 8 - pad + k)
            slabs.append(_silu(acc))
        return slabs[0] if len(slabs) == 1 else jnp.concatenate(slabs, axis=1)

    def real_time(pos):
        return ((pos & (nph - 1)) << 3) | (pos >> (nph.bit_length() - 1))

    def front(g):
        xg = conv_silu(g * gw, gw)
        bb = conv_silu(di + g * ds, ds).astype(BF16)
        cc = conv_silu(di + gn + g * ds, ds).astype(BF16)
        cbm = _dot_nt(cc, bb)
        hg = hst[g * gw:(g + 1) * gw, :]
        yoff = _dot_nt(cc, hg.astype(BF16))
        e_g = e_ref[:, g * gw:(g + 1) * gw]
        w1g = _dot(wsp[0], e_g)
        w2g = _dot(wsp[1], e_g)
        return xg, bb, cbm, hg, yoff, w1g, w2g

    dt = _softplus(phase_rows(dtpad, 0) + dtb_ref[...])
    if ts < q:
        dt = jnp.where(real_time(lax.broadcasted_iota(jnp.int32, (q, LANES), 0)) < ts, dt, 0.0)
    a = dt * (-jnp.exp(alog_ref[...]))
    causal = (real_time(lax.broadcasted_iota(jnp.int32, (q, q), 0))
              >= real_time(lax.broadcasted_iota(jnp.int32, (q, q), 1)))
    tril = jnp.where(causal, 1.0, 0.0).astype(BF16)

    a_hi, a_lo = _split_bf16(a)
    acs = _dot(jnp.concatenate([tril, tril], axis=1),
               jnp.concatenate([a_hi, a_lo], axis=0))
    acs2 = acs * LOG2E
    acs2_t = acs2.T
    dt_t = dt.T
    acs_last = acs[q - 1:q, :]
    wsp[0] = jnp.concatenate(_split_bf16(dt * jnp.exp(acs_last - acs)), axis=1)
    wsp[1] = jnp.concatenate(_split_bf16(jnp.exp(acs)), axis=1)
    cd = jnp.exp(acs_last)
    lane_g = lax.broadcasted_iota(jnp.int32, (1, gw), 1)

    nxt = front(0)
    for g in range(ng):
        xg, bb, cbm, hg, yoff, w1g, w2g = nxt
        if g + 1 < ng:
            nxt = front(g + 1)
        xgb = xg.astype(BF16)
        yd = jnp.zeros((q, gw), F32)
        for r in range(hpg):
            h = g * hpg + r
            seg = acs2[:, h:h + 1] - acs2_t[h:h + 1, :]
            dec = jnp.exp2(jnp.where(causal, seg, NEG))
            mh = (cbm * dec * dt_t[h:h + 1, :]).astype(BF16)
            yd = jnp.where((lane_g >= r * hd) & (lane_g < (r + 1) * hd), _dot(mh, xgb), yd)
        yg = yd + yoff * w2g + dsk_ref[:, g * gw:(g + 1) * gw] * xg
        for jj in range(gw // LANES):
            slab = yg[:, jj * LANES:(jj + 1) * LANES]
            cy = g * (gw // LANES) + jj
            if ts == q:
                for ph in range(8):
                    y_ref[0, cy, pl.ds(ph, nph, stride=8), :] = slab[ph * nph:(ph + 1) * nph]
            else:
                for tt in range(ts):
                    pos = (tt % 8) * nph + tt // 8
                    y_ref[0, cy, tt:tt + 1, :] = slab[pos:pos + 1]
        st = _dot((xg * w1g).T.astype(BF16), bb)
        cdt = jnp.concatenate(
            [jnp.broadcast_to(cd[:, g * hpg + r:g * hpg + r + 1], (hd, ds)) for r in range(hpg)], axis=0)
        hst[g * gw:(g + 1) * gw, :] = hg * cdt + st

    for c in range(ncs):
        xpad[c, 8 - pad:8, :] = xpad[c, 8 + ts - pad:8 + ts, :]

    @pl.when(s == pl.num_programs(1) - 1)
    def _():
        hn_ref[nprev, 0] = hst[...]
        if nprev:
            prev_h, prev_c = prev_refs
            hn_ref[0:nprev] = prev_h[...]
            convn_ref[0:nprev] = prev_c[...]


def _ssd(zxbc3, conv0, h0, layer, cw, cb, dtb, alog, dsk, emat, prev, *, nh, hd, ng, ds):
    nb, length, _ = zxbc3.shape
    di = nh * hd
    cdim = di + 2 * ng * ds
    ts = min(CHUNK, length)
    ns = length // ts
    nprev = prev[0].shape[0] if prev else 0
    body = functools.partial(_ssd_body, ts=ts, nh=nh, hd=hd, ng=ng, ds=ds, nprev=nprev)
    assert cdim == 2 * di
    dt_blk = (di + cdim) // LANES
    prev_specs = [
        pl.BlockSpec((nprev, 1, di, ds), lambda b, s: (0, b, 0, 0)),
        pl.BlockSpec((nprev, 1, CONV_W - 1, cdim), lambda b, s: (0, b, 0, 0)),
    ] if prev else []
    return pl.pallas_call(
        body,
        grid=(nb, ns),
        in_specs=[
            pl.BlockSpec((1, ts, di), lambda b, s: (b, s, 1)),
            pl.BlockSpec((1, ts, di), lambda b, s: (b, s, 2)),
            pl.BlockSpec((1, ts, LANES), lambda b, s: (b, s, dt_blk)),
            pl.BlockSpec((1, 1, CONV_W - 1, cdim), lambda b, s: (layer, b, 0, 0)),
            pl.BlockSpec((1, 1, di, ds), lambda b, s: (layer, b, 0, 0)),
            pl.BlockSpec((CONV_W, cdim), lambda b, s: (0, 0)),
            pl.BlockSpec((1, cdim), lambda b, s: (0, 0)),
            pl.BlockSpec((1, LANES), lambda b, s: (0, 0)),
            pl.BlockSpec((1, LANES), lambda b, s: (0, 0)),
            pl.BlockSpec((1, di), lambda b, s: (0, 0)),
            pl.BlockSpec((2 * LANES, di), lambda b, s: (0, 0)),
        ] + prev_specs,
        out_specs=[
            pl.BlockSpec((1, di // LANES, ts, LANES), lambda b, s: (b, 0, s, 0)),
            pl.BlockSpec((nprev + 1, 1, di, ds), lambda b, s: (0, b, 0, 0)),
            pl.BlockSpec((nprev + 1, 1, CONV_W - 1, cdim), lambda b, s: (0, b, 0, 0)),
        ],
        out_shape=[SDS((nb, di // LANES, length, LANES), F32), SDS((nprev + 1, nb, di, ds), F32),
                   SDS((nprev + 1, nb, CONV_W - 1, cdim), F32)],
        scratch_shapes=[
            pltpu.VMEM((cdim // LANES, 8 + CHUNK, LANES), F32),
            pltpu.VMEM((CHUNK, LANES), F32),
            pltpu.VMEM((di, ds), F32),
            pltpu.VMEM((2, CHUNK, 2 * LANES), BF16),
        ],
        compiler_params=_cparams(("parallel", "arbitrary")),
        name="ssd",
    )(zxbc3, zxbc3, zxbc3, conv0, h0, cw, cb, dtb, alog, dsk, emat, *prev)


def _post_body(y_ref, z_ref, x_ref, gw_ref, w_ref, o_ref, *, ng):
    if len(y_ref.shape) == 4:
        y = jnp.concatenate([y_ref[0, c] for c in range(y_ref.shape[1])], axis=1)
    else:
        y = y_ref[...]
    yz = y * _silu(z_ref[...])
    width = yz.shape[1] // ng
    parts = []
    for g in range(ng):
        blk = yz[:, g * width:(g + 1) * width]
        ms = jnp.mean(blk * blk, axis=-1, keepdims=True)
        parts.append(blk * lax.rsqrt(ms + EPS))
    yn = jnp.concatenate(parts, axis=1) * gw_ref[...]
    o_ref[...] = x_ref[...] + _dot(yn.astype(BF16), w_ref[0].astype(BF16))


def _post(y, zxbc2, x2, gw, w_out, layer, tm, ng):
    m, d = x2.shape
    if y.ndim == 4:
        nslab, length = y.shape[1], y.shape[2]
        di = nslab * LANES
        per_seq = length // tm
        y_spec = pl.BlockSpec((1, nslab, tm, LANES), lambda i: (i // per_seq, 0, i % per_seq, 0))
    else:
        di = y.shape[1]
        y_spec = pl.BlockSpec((tm, di), lambda i: (i, 0))
    return pl.pallas_call(
        functools.partial(_post_body, ng=ng),
        grid=(m // tm,),
        in_specs=[
            y_spec,
            pl.BlockSpec((tm, di), lambda i: (i, 0)),
            pl.BlockSpec((tm, d), lambda i: (i, 0)),
            pl.BlockSpec((1, di), lambda i: (0, 0)),
            pl.BlockSpec((1, di, d), lambda i: (layer, 0, 0)),
        ],
        out_specs=pl.BlockSpec((tm, d), lambda i: (i, 0)),
        out_shape=SDS((m, d), F32),
        compiler_params=_cparams(("parallel",)),
        name="post_mamba",
    )(y, zxbc2, x2, gw, w_out)


def _mlp_body(x_ref, g_ref, wu_ref, wd_ref, o_ref, u_scr):
    _mlp_prologue(x_ref, g_ref, u_scr)
    _mlp_epilogue(x_ref, o_ref, _mlp_main(wu_ref, wd_ref, u_scr))


def _mlp_prologue(x_ref, g_ref, u_scr):
    @pl.when(pl.program_id(1) == 0)
    def _():
        u_scr[...] = _rms(x_ref[...], g_ref[...]).astype(BF16)


def _mlp_main(wu_ref, wd_ref, u_scr):
    u = u_scr[...]
    half = wu_ref.shape[2] // 2
    hs = [jnp.maximum(_dot(u, wu_ref[0, :, k * half:(k + 1) * half].astype(BF16)), 0.0) for k in range(2)]
    part = _dot((hs[0] * hs[0]).astype(BF16), wd_ref[0, 0:half, :].astype(BF16))
    return part + _dot((hs[1] * hs[1]).astype(BF16), wd_ref[0, half:2 * half, :].astype(BF16))


def _mlp_epilogue(x_ref, o_ref, part):
    c = pl.program_id(1)

    @pl.when(c == 0)
    def _():
        o_ref[...] = x_ref[...] + part

    @pl.when(c != 0)
    def _():
        o_ref[...] += part


def _mlp(x2, g, w_up, w_down, layer, tm, tf):
    m, d = x2.shape
    ff = w_up.shape[2]
    return pl.pallas_call(
        _mlp_body,
        grid=(m // tm, ff // tf),
        in_specs=[
            pl.BlockSpec((tm, d), lambda i, c: (i, 0)),
            pl.BlockSpec((1, d), lambda i, c: (0, 0)),
            pl.BlockSpec((1, d, tf), lambda i, c: (layer, 0, c)),
            pl.BlockSpec((1, tf, d), lambda i, c: (layer, c, 0)),
        ],
        out_specs=pl.BlockSpec((tm, d), lambda i, c: (i, 0)),
        out_shape=SDS((m, d), F32),
        scratch_shapes=[pltpu.VMEM((tm, d), BF16)],
        compiler_params=pltpu.CompilerParams(dimension_semantics=("parallel", "arbitrary"),
                                             vmem_limit_bytes=MLP_VMEM_LIMIT),
        name="mlp",
    )(x2, g, w_up, w_down)


def _row_group_norm(xt, width):
    n, tm = xt.shape
    x3 = xt.reshape(n // width, width, tm)
    ms = jnp.mean(x3 * x3, axis=1, keepdims=True)
    return (x3 * lax.rsqrt(ms + EPS)).reshape(n, tm)


def _kv_body(x_ref, g_ref, wkt_ref, wv_ref, kn_ref, kt_ref, v_ref, *head_refs, nha, ahd):
    u = _rms(x_ref[0], g_ref[...]).astype(BF16)
    kt = _row_group_norm(_dot_nt(wkt_ref[...], u), ahd) * kn_ref[...]
    v = _dot(u, wv_ref[...])
    kt_ref[0] = kt
    v_ref[0] = v
    if head_refs:
        kh_ref, vht_ref = head_refs
        k = kt.T
        for h in range(nha):
            kh_ref[0, h] = k[:, h * LANES:(h + 1) * LANES].astype(BF16)
            vht_ref[0, h] = v[:, h * LANES:(h + 1) * LANES].T.astype(BF16)


def _kv(x3, g, wk_t, wv, kn_col, tm, nha, ahd, head_major):
    nb, t, d = x3.shape
    kd = nha * LANES
    out_specs = [
        pl.BlockSpec((1, kd, tm), lambda b, i: (b, 0, i)),
        pl.BlockSpec((1, tm, kd), lambda b, i: (b, i, 0)),
    ]
    out_shape = [SDS((nb, kd, t), F32), SDS((nb, t, kd), F32)]
    if head_major:
        out_specs += [
            pl.BlockSpec((1, nha, tm, LANES), lambda b, i: (b, 0, i, 0)),
            pl.BlockSpec((1, nha, LANES, tm), lambda b, i: (b, 0, 0, i)),
        ]
        out_shape += [SDS((nb, nha, t, LANES), BF16), SDS((nb, nha, LANES, t), BF16)]
    return pl.pallas_call(
        functools.partial(_kv_body, nha=nha, ahd=ahd),
        grid=(nb, t // tm),
        in_specs=[
            pl.BlockSpec((1, tm, d), lambda b, i: (b, i, 0)),
            pl.BlockSpec((1, d), lambda b, i: (0, 0)),
            pl.BlockSpec((kd, d), lambda b, i: (0, 0)),
            pl.BlockSpec((d, kd), lambda b, i: (0, 0)),
            pl.BlockSpec((kd, 1), lambda b, i: (0, 0)),
        ],
        out_specs=out_specs,
        out_shape=out_shape,
        compiler_params=_cparams(("parallel", "parallel")),
        name="kv_proj",
    )(x3, g, wk_t, wv, kn_col)


def _q_body(x_ref, g_ref, w_ref, qn_ref, gm_ref, q_ref, *, scale):
    u = _rms(x_ref[0], g_ref[...]).astype(BF16)
    qn = _group_norm64(_dot(u, w_ref[0].astype(BF16)), gm_ref[...]) * qn_ref[...] * scale
    q_ref[0] = qn.astype(BF16)


def _qproj(x3, g, w_q, layer, qn_t, gmat, tm, nha, scale):
    nb, t, d = x3.shape
    kd = nha * LANES
    return pl.pallas_call(
        functools.partial(_q_body, scale=scale),
        grid=(nb, t // tm),
        in_specs=[
            pl.BlockSpec((1, tm, d), lambda b, i: (b, i, 0)),
            pl.BlockSpec((1, d), lambda b, i: (0, 0)),
            pl.BlockSpec((1, d, kd), lambda b, i: (layer, 0, 0)),
            pl.BlockSpec((1, kd), lambda b, i: (0, 0)),
            pl.BlockSpec((256, 256), lambda b, i: (0, 0)),
        ],
        out_specs=pl.BlockSpec((1, tm, kd), lambda b, i: (b, i, 0)),
        out_shape=SDS((nb, t, kd), BF16),
        compiler_params=_cparams(("parallel", "parallel")),
        name="q_proj",
    )(x3, g, w_q, qn_t, gmat)


def _lam(l1, k1, l2, k2, lam_init):
    return (jnp.exp(jnp.sum(l1 * k1, axis=-1, keepdims=True))
            - jnp.exp(jnp.sum(l2 * k2, axis=-1, keepdims=True)) + lam_init)


def _attn_body(it_ref, jt_ref, x_ref, g_ref, wqt_ref, qn_ref, k_ref, vt_ref,
               l1_ref, k1_ref, l2_ref, k2_ref, sw_ref, wo_ref, y_ref,
               qt_ref, m_scr, l_scr, acc_scr, st_scr, mx_scr, *, tq, tk, nha, ahd, qscale, lam_init):
    t = pl.program_id(1)
    i = it_ref[t]
    j = jt_ref[t]

    @pl.when(j == 0)
    def _():
        m_scr[...] = jnp.full_like(m_scr, NEG)
        l_scr[...] = jnp.zeros_like(l_scr)
        acc_scr[...] = jnp.zeros_like(acc_scr)
        u = _rms(x_ref[0], g_ref[...]).astype(BF16)
        qt = _row_group_norm(_dot_nt(wqt_ref[0], u), ahd) * (qn_ref[...] * qscale)
        rows = lax.broadcasted_iota(jnp.int32, (LANES, tq), 0)
        for h in range(nha):
            blk = qt[h * LANES:(h + 1) * LANES, :]
            qt_ref[h, 0] = jnp.where(rows < ahd, blk, 0.0).astype(BF16)
            qt_ref[h, 1] = jnp.where(rows >= ahd, blk, 0.0).astype(BF16)

    def scores(h, c, slot, bias):
        st = _dot(k_ref[0, h], qt_ref[h, c])
        if bias is not None:
            st = st + bias
        st_scr[slot] = st
        mx_scr[slot] = jnp.max(st, axis=0, keepdims=True)

    def accumulate(h, c, slot):
        m_prev = m_scr[h, c]
        m_new = jnp.maximum(m_prev, mx_scr[slot])
        alpha = jnp.exp2(m_prev - m_new)
        p = jnp.exp2(st_scr[slot] - m_new)
        l_scr[h, c] = alpha * l_scr[h, c] + jnp.sum(p, axis=0, keepdims=True)
        acc_scr[h, c] = alpha * acc_scr[h, c] + _dot(vt_ref[0, h], p.astype(BF16))
        m_scr[h, c] = m_new

    def run_heads(bias):
        scores(0, 0, 0, bias)

        def head(h, carry):
            scores(h, 1, 1, bias)
            accumulate(h, 0, 0)
            scores(jnp.minimum(h + 1, nha - 1), 0, 0, bias)
            accumulate(h, 1, 1)
            return carry

        lax.fori_loop(0, nha, head, 0, unroll=True)

    needs_mask = (j + 1) * tk - 1 > i * tq

    @pl.when(needs_mask)
    def _():
        kpos = j * tk + lax.broadcasted_iota(jnp.int32, (tk, tq), 0)
        qpos = i * tq + lax.broadcasted_iota(jnp.int32, (tk, tq), 1)
        run_heads(jnp.where(kpos <= qpos, 0.0, NEG))

    @pl.when(jnp.logical_not(needs_mask))
    def _():
        run_heads(None)

    @pl.when((j + 1) * tk >= (i + 1) * tq)
    def _():
        lam = _lam(l1_ref[...], k1_ref[...], l2_ref[...], k2_ref[...], lam_init)
        parts = []
        for h in range(nha):
            ot = (acc_scr[h, 0] * (1.0 / l_scr[h, 0])
                  - (lam * (1.0 / l_scr[h, 1])) * acc_scr[h, 1])
            ms = jnp.mean(ot * ot, axis=0, keepdims=True)
            parts.append((ot * lax.rsqrt(ms + EPS)).T)
        on = jnp.concatenate(parts, axis=1) * (sw_ref[...] * (1.0 - lam_init))
        y_ref[0] = x_ref[0] + _dot(on.astype(BF16), wo_ref[0].astype(BF16))


def _attn_prompt(x3, g, wq_t, qn_col, kh, vht, l1, k1, l2, k2, sw_t, w_o, layer, tq, tk, ahd, qscale,
                 lam_init):
    nb, t, d = x3.shape
    nha = kh.shape[1]
    kd = nha * LANES
    hd = ahd
    its, jts = [], []
    for i in range(t // tq):
        for j in range(((i + 1) * tq + tk - 1) // tk):
            its.append(i)
            jts.append(j)
    it = jnp.asarray(its, jnp.int32)
    jt = jnp.asarray(jts, jnp.int32)
    lspec = pl.BlockSpec((1, hd), lambda b, s, it, jt: (0, 0))
    grid_spec = pltpu.PrefetchScalarGridSpec(
        num_scalar_prefetch=2,
        grid=(nb, len(its)),
        in_specs=[
            pl.BlockSpec((1, tq, d), lambda b, s, it, jt: (b, it[s], 0)),
            pl.BlockSpec((1, d), lambda b, s, it, jt: (0, 0)),
            pl.BlockSpec((1, kd, d), lambda b, s, it, jt: (layer, 0, 0)),
            pl.BlockSpec((kd, 1), lambda b, s, it, jt: (0, 0)),
            pl.BlockSpec((1, nha, tk, LANES), lambda b, s, it, jt: (b, 0, jt[s], 0)),
            pl.BlockSpec((1, nha, LANES, tk), lambda b, s, it, jt: (b, 0, 0, jt[s])),
            lspec, lspec, lspec, lspec,
            pl.BlockSpec((1, kd), lambda b, s, it, jt: (0, 0)),
            pl.BlockSpec((1, kd, d), lambda b, s, it, jt: (layer, 0, 0)),
        ],
        out_specs=pl.BlockSpec((1, tq, d), lambda b, s, it, jt: (b, it[s], 0)),
        scratch_shapes=[
            pltpu.VMEM((nha, 2, LANES, tq), BF16),
            pltpu.VMEM((nha, 2, 1, tq), F32),
            pltpu.VMEM((nha, 2, 1, tq), F32),
            pltpu.VMEM((nha, 2, LANES, tq), F32),
            pltpu.VMEM((2, tk, tq), F32),
            pltpu.VMEM((2, 1, tq), F32),
        ],
    )
    return pl.pallas_call(
        functools.partial(_attn_body, tq=tq, tk=tk, nha=nha, ahd=ahd, qscale=qscale, lam_init=lam_init),
        grid_spec=grid_spec,
        out_shape=SDS((nb, t, d), F32),
        compiler_params=_cparams(("parallel", "arbitrary")),
        name="attn_prompt",
    )(it, jt, x3, g, wq_t, qn_col, kh, vht, l1, k1, l2, k2, sw_t, w_o)


def _decode_body(pt_ref, q_ref, kn_ref, vn_ref, l1_ref, k1_ref, l2_ref, k2_ref, *rest,
                 pps, tn, nha, hd, lam_init):
    _decode_step(pl.program_id(0), pl.program_id(1), pl.num_programs(1), q_ref, kn_ref, vn_ref,
                 (l1_ref, k1_ref, l2_ref, k2_ref), rest[:pps], rest[pps:2 * pps], rest[2 * pps],
                 rest[2 * pps + 1:], tn=tn, nha=nha, hd=hd, lam_init=lam_init)


def _decode_step(b_idx, j, nj, q_ref, kn_ref, vn_ref, lam_refs, k_refs, v_refs, o_ref, scratch,
                 *, tn, nha, hd, lam_init, phases=("setup", "pages", "finish")):
    l1_ref, k1_ref, l2_ref, k2_ref = lam_refs
    qb, m_scr, l_scr, acc_scr = scratch
    pps = len(k_refs)
    nmap = 2 * nha
    nrow = tn * nmap
    kd = nha * LANES
    page = k_refs[0].shape[2]

    def online(sc, vals):
        n = len(vals)
        m_prev = m_scr[...]
        blk_max = sc[:, 0:LANES]
        for b in range(1, n):
            blk_max = jnp.maximum(blk_max, sc[:, b * LANES:(b + 1) * LANES])
        m_new = jnp.maximum(m_prev, jnp.max(blk_max, axis=-1, keepdims=True))
        alpha = jnp.exp(m_prev - m_new)
        p = jnp.exp(sc - jnp.concatenate([m_new] * n, axis=1))
        psum = p[:, 0:LANES]
        for b in range(1, n):
            psum = psum + p[:, b * LANES:(b + 1) * LANES]
        l_scr[...] = alpha * l_scr[...] + psum
        pv = None
        for b in range(0, n, 2):
            nb2 = min(2, n - b)
            vblk = vals[b] if nb2 == 1 else jnp.concatenate(vals[b:b + 2], axis=0)
            term = _dot(p[:, b * LANES:(b + nb2) * LANES].astype(BF16), vblk)
            pv = term if pv is None else pv + term
        acc_scr[...] = jnp.concatenate([alpha] * (kd // LANES), axis=1) * acc_scr[...] + pv
        m_scr[...] = m_new

    def page_values(v_ref):
        return jnp.concatenate(
            [v_ref[0, pl.ds(h, page, stride=nha), :] for h in range(nha)], axis=1).astype(BF16)

    def when(phase, cond):
        return pl.when(cond) if phase in phases else (lambda fn: None)

    @when("setup", j == 0)
    def _():
        rid = lax.broadcasted_iota(jnp.int32, (nmap, kd), 0)
        lid = lax.broadcasted_iota(jnp.int32, (nmap, kd), 1)
        sel = (lid >= rid * hd) & (lid < (rid + 1) * hd)
        qf = q_ref[0].astype(F32)
        for tt in range(tn):
            rowq = jnp.broadcast_to(qf[tt:tt + 1, :], (nmap, kd))
            qb[tt * nmap:(tt + 1) * nmap, :] = jnp.where(sel, rowq, 0.0).astype(BF16)
        m_scr[...] = jnp.full_like(m_scr, NEG)
        l_scr[...] = jnp.zeros_like(l_scr)
        acc_scr[...] = jnp.zeros_like(acc_scr)
        nnew = kn_ref.shape[2]
        qrow = lax.broadcasted_iota(jnp.int32, (nrow, LANES), 0)
        ktok = lax.broadcasted_iota(jnp.int32, (nrow, LANES), 1) - b_idx * tn
        for blk in range(nnew // LANES):
            sc = _dot(qb[...], kn_ref[0, :, blk * LANES:(blk + 1) * LANES].astype(BF16))
            kt = ktok - blk * LANES
            sc = jnp.where((kt >= 0) & (kt * nmap <= qrow), sc, NEG)
            online(sc, [vn_ref[0, blk * LANES:(blk + 1) * LANES, :].astype(BF16)])

    if "pages" in phases:
        scs = [_dot(qb[...], jnp.concatenate([k_refs[b + i][0].astype(BF16) for i in range(min(2, pps - b))],
                                             axis=1)) for b in range(0, pps, 2)]
        online(jnp.concatenate(scs, axis=1), [page_values(v_refs[b]) for b in range(pps)])

    @when("finish", j == nj - 1)
    def _():
        lam = _lam(l1_ref[...], k1_ref[...], l2_ref[...], k2_ref[...], lam_init)
        inv_l = 1.0 / jnp.sum(l_scr[...], axis=-1, keepdims=True)
        rid = lax.broadcasted_iota(jnp.int32, (nmap, 1), 0)
        rid2 = lax.broadcasted_iota(jnp.int32, (nmap, kd), 0)
        lid2 = lax.broadcasted_iota(jnp.int32, (nmap, kd), 1)
        own = (lid2 >= (rid2 >> 1) * LANES) & (lid2 < ((rid2 >> 1) + 1) * LANES)
        for tt in range(tn):
            il = inv_l[tt * nmap:(tt + 1) * nmap, :]
            coef = jnp.where((rid & 1) == 0, il, -lam * il)
            w = jnp.where(own, acc_scr[tt * nmap:(tt + 1) * nmap, :] * coef, 0.0)
            o_ref[0, tt:tt + 1, :] = jnp.sum(w, axis=0, keepdims=True)


def _attn_sample(page_table, q3, kt_new, v_new, cache_kt, cache_v2, l1, k1, l2, k2, pps, hd, lam_init):
    nb, tn, kd = q3.shape
    nha = kd // LANES
    n_pages = page_table.shape[1]
    page = cache_kt.shape[2]
    nnew = kt_new.shape[2]
    nrow = tn * 2 * nha
    assert page == LANES and nnew == LANES and nb * tn == nnew
    pt_flat = page_table.reshape(-1)

    def k_spec(b_off):
        return pl.BlockSpec((1, kd, page),
                            lambda b, j, pt: (pt[b * n_pages + j * pps + b_off], 0, 0))

    def v_spec(b_off):
        return pl.BlockSpec((1, page * nha, LANES),
                            lambda b, j, pt: (pt[b * n_pages + j * pps + b_off], 0, 0))

    lspec = pl.BlockSpec((1, hd), lambda b, j, pt: (0, 0))
    grid_spec = pltpu.PrefetchScalarGridSpec(
        num_scalar_prefetch=1,
        grid=(nb, n_pages // pps),
        in_specs=[
            pl.BlockSpec((1, tn, kd), lambda b, j, pt: (b, 0, 0)),
            pl.BlockSpec((1, kd, nnew), lambda b, j, pt: (0, 0, 0)),
            pl.BlockSpec((1, nnew, kd), lambda b, j, pt: (0, 0, 0)),
            lspec, lspec, lspec, lspec]
        + [k_spec(b) for b in range(pps)] + [v_spec(b) for b in range(pps)],
        out_specs=pl.BlockSpec((1, tn, kd), lambda b, j, pt: (b, 0, 0)),
        scratch_shapes=[
            pltpu.VMEM((nrow, kd), BF16),
            pltpu.VMEM((nrow, LANES), F32),
            pltpu.VMEM((nrow, LANES), F32),
            pltpu.VMEM((nrow, kd), F32),
        ],
    )
    return pl.pallas_call(
        functools.partial(_decode_body, pps=pps, tn=tn, nha=nha, hd=hd, lam_init=lam_init),
        grid_spec=grid_spec,
        out_shape=SDS((nb, tn, kd), F32),
        compiler_params=_cparams(("parallel", "arbitrary")),
        name="attn_sample",
    )(pt_flat, q3, kt_new, v_new, l1, k1, l2, k2, *([cache_kt] * pps), *([cache_v2] * pps))


def _mlp_decode_body(pt_ref, x_ref, g_ref, wu_ref, wd_ref, q_ref, kn_ref, vn_ref,
                     l1_ref, k1_ref, l2_ref, k2_ref, *rest, pps, spp, seq0, tn, nha, hd, lam_init):
    k_refs = rest[:pps]
    v_refs = rest[pps:2 * pps]
    o_ref, a_ref = rest[2 * pps], rest[2 * pps + 1]
    u_scr = rest[2 * pps + 2]
    step = pl.program_id(0) * pl.num_programs(1) + pl.program_id(1)
    decode = functools.partial(
        _decode_step, seq0 + step // spp, step % spp, spp, q_ref, kn_ref, vn_ref,
        (l1_ref, k1_ref, l2_ref, k2_ref), k_refs, v_refs, a_ref, rest[2 * pps + 3:],
        tn=tn, nha=nha, hd=hd, lam_init=lam_init)
    _mlp_prologue(x_ref, g_ref, u_scr)
    decode(phases=("setup",))
    part = _mlp_main(wu_ref, wd_ref, u_scr)
    decode(phases=("pages",))
    _mlp_epilogue(x_ref, o_ref, part)
    decode(phases=("finish",))


def _mlp_decode(x2, g, w_up, w_down, layer, tm, tf, page_table, q3, kt_new, v_new, cache_kt, cache_v2,
                l1, k1, l2, k2, seq0, nseq, pps, hd, lam_init):
    m, d = x2.shape
    ff = w_up.shape[2]
    _, tn, kd = q3.shape
    nha = kd // LANES
    n_pages = page_table.shape[1]
    page = cache_kt.shape[2]
    nnew = kt_new.shape[2]
    nrow = tn * 2 * nha
    spp = n_pages // pps
    nc = ff // tf
    assert (m // tm) * nc == nseq * spp and page == LANES and nnew == LANES
    pt_flat = page_table.reshape(-1)

    def seq_of(i, c):
        return seq0 + (i * nc + c) // spp

    def page_of(i, c, pt, b_off):
        return pt[seq_of(i, c) * n_pages + ((i * nc + c) % spp) * pps + b_off]

    def k_spec(b_off):
        return pl.BlockSpec((1, kd, page), lambda i, c, pt: (page_of(i, c, pt, b_off), 0, 0))

    def v_spec(b_off):
        return pl.BlockSpec((1, page * nha, LANES), lambda i, c, pt: (page_of(i, c, pt, b_off), 0, 0))

    lspec = pl.BlockSpec((1, hd), lambda i, c, pt: (0, 0))
    grid_spec = pltpu.PrefetchScalarGridSpec(
        num_scalar_prefetch=1,
        grid=(m // tm, nc),
        in_specs=[
            pl.BlockSpec((tm, d), lambda i, c, pt: (i, 0)),
            pl.BlockSpec((1, d), lambda i, c, pt: (0, 0)),
            pl.BlockSpec((1, d, tf), lambda i, c, pt: (layer, 0, c)),
            pl.BlockSpec((1, tf, d), lambda i, c, pt: (layer, c, 0)),
            pl.BlockSpec((1, tn, kd), lambda i, c, pt: (seq_of(i, c), 0, 0)),
            pl.BlockSpec((1, kd, nnew), lambda i, c, pt: (0, 0, 0)),
            pl.BlockSpec((1, nnew, kd), lambda i, c, pt: (0, 0, 0)),
            lspec, lspec, lspec, lspec]
        + [k_spec(b) for b in range(pps)] + [v_spec(b) for b in range(pps)],
        out_specs=[
            pl.BlockSpec((tm, d), lambda i, c, pt: (i, 0)),
            pl.BlockSpec((1, tn, kd), lambda i, c, pt: (seq_of(i, c) - seq0, 0, 0)),
        ],
        scratch_shapes=[
            pltpu.VMEM((tm, d), BF16),
            pltpu.VMEM((nrow, kd), BF16),
            pltpu.VMEM((nrow, LANES), F32),
            pltpu.VMEM((nrow, LANES), F32),
            pltpu.VMEM((nrow, kd), F32),
        ],
    )
    return pl.pallas_call(
        functools.partial(_mlp_decode_body, pps=pps, spp=spp, seq0=seq0, tn=tn, nha=nha, hd=hd,
                          lam_init=lam_init),
        grid_spec=grid_spec,
        out_shape=[SDS((m, d), F32), SDS((nseq, tn, kd), F32)],
        compiler_params=pltpu.CompilerParams(dimension_semantics=("arbitrary", "arbitrary"),
                                             vmem_limit_bytes=MLP_VMEM_LIMIT),
        name="mlp_decode",
    )(pt_flat, x2, g, w_up, w_down, q3, kt_new, v_new, l1, k1, l2, k2,
      *([cache_kt] * pps), *([cache_v2] * pps))


def _attn_out_body(o_ref, x_ref, sw_ref, w_ref, y_ref, *, nha, post_scale):
    o = o_ref[...]
    parts = []
    for h in range(nha):
        blk = o[:, h * LANES:(h + 1) * LANES]
        ms = jnp.mean(blk * blk, axis=-1, keepdims=True)
        parts.append(blk * lax.rsqrt(ms + EPS))
    on = jnp.concatenate(parts, axis=1) * sw_ref[...] * post_scale
    y_ref[...] = x_ref[...] + _dot(on.astype(BF16), w_ref[0].astype(BF16))


def _attn_out(o2, x2, sw_t, w_o, layer, tm, nha, post_scale):
    m, d = x2.shape
    vd = o2.shape[1]
    return pl.pallas_call(
        functools.partial(_attn_out_body, nha=nha, post_scale=post_scale),
        grid=(m // tm,),
        in_specs=[
            pl.BlockSpec((tm, vd), lambda i: (i, 0)),
            pl.BlockSpec((tm, d), lambda i: (i, 0)),
            pl.BlockSpec((1, vd), lambda i: (0, 0)),
            pl.BlockSpec((1, vd, d), lambda i: (layer, 0, 0)),
        ],
        out_specs=pl.BlockSpec((tm, d), lambda i: (i, 0)),
        out_shape=SDS((m, d), F32),
        compiler_params=_cparams(("parallel",)),
        name="attn_out",
    )(o2, x2, sw_t, w_o)


def kernel(x_prompt, x_sample, state_ssm, state_conv, cache_k, cache_v, page_table, norm_a, w_in_a, conv_w, conv_b, dt_bias, a_log, d_skip, gnorm_w, w_out_a, kv_norm, w_kv, k_norm, norm_b, w_q, q_norm, lam_q1, lam_k1, lam_q2, lam_k2, subln, w_o, norm_m, w_up, w_down):
    n_a = norm_a.shape[0]
    n_b = norm_b.shape[0]
    d = x_prompt.shape[-1]
    nh, hd, ds = state_ssm.shape[2], state_ssm.shape[3], state_ssm.shape[4]
    di = nh * hd
    cdim = state_conv.shape[-1]
    ng = (cdim - di) // (2 * ds)
    nha, ahd = cache_k.shape[2], cache_k.shape[4]
    kd = nha * 2 * ahd
    page = cache_k.shape[1]
    assert 2 * ahd == LANES and cache_v.shape[3] == LANES and nh <= LANES

    emat = (jnp.arange(LANES)[:, None] == (jnp.arange(di)[None, :] // hd)).astype(BF16)
    emat = jnp.concatenate([emat, emat], axis=0)
    gidx = jnp.arange(256) // ahd
    gmat = jnp.where(gidx[:, None] == gidx[None, :], 1.0 / ahd, 0.0).astype(BF16)

    def row(v):
        return v.reshape(1, -1).astype(F32)

    def pad_lanes(v):
        return jnp.pad(v.astype(F32), (0, LANES - v.shape[0])).reshape(1, LANES)

    n_in = di + cdim + nh
    tn_in = -(-n_in // (2 * LANES)) * LANES
    w_in_b = jnp.pad(w_in_a, ((0, 0), (0, 0), (0, 2 * tn_in - n_in))).astype(BF16)
    wk_t_b = w_kv[:, :kd].T.astype(BF16)
    wv_b = w_kv[:, kd:].astype(BF16)
    wq_t_b = jnp.swapaxes(w_q, 1, 2).astype(BF16)
    scale = ahd ** -0.5

    def trunk(x3, conv0, ssm0, seq_major):
        nb, t, _ = x3.shape
        m = nb * t
        tm = min(512, m)
        x2 = x3.reshape(m, d)
        ab, at = (nb, t) if seq_major else (1, m)
        tma = min(512, at)
        states = ()
        k_new = v_new = kv_ctx = None
        for l in range(n_a + n_b):
            if l < n_a:
                zxbc = _inproj(x2, row(norm_a[l]), w_in_b, l, tm, tn_in)
                y3, hn, cn = _ssd(
                    zxbc.reshape(nb, t, 2 * tn_in), conv0, ssm0.reshape(n_a, nb, di, ds), l,
                    conv_w[l], row(conv_b[l]), pad_lanes(dt_bias[l]), pad_lanes(a_log[l]),
                    row(jnp.repeat(d_skip[l], hd)), emat, states, nh=nh, hd=hd, ng=ng, ds=ds)
                states = (hn, cn)
                tmp = min(256, m)
                if t % tmp:
                    y3 = jnp.transpose(y3, (0, 2, 1, 3)).reshape(m, di)
                x2 = _post(y3, zxbc, x2, row(gnorm_w[l]), w_out_a, l, tmp, ng)
            else:
                jb = l - n_a
                if l == n_a:
                    kv_ctx = _kv(x2.reshape(ab, at, d), row(kv_norm), wk_t_b, wv_b,
                                 jnp.tile(k_norm, 2 * nha).reshape(kd, 1).astype(F32), tma, nha, ahd, seq_major)
                    k_new = jnp.transpose(kv_ctx[0].reshape(ab, nha, 2, ahd, at), (0, 4, 1, 2, 3))
                    k_new = k_new.reshape(nb, t, nha, 2, ahd)
                    v_new = kv_ctx[1].reshape(nb, t, nha, LANES)
                lam_init = 0.8 - 0.6 * math.exp(-0.3 * l)
                lams = (row(lam_q1[jb]), row(lam_k1[jb]), row(lam_q2[jb]), row(lam_k2[jb]))
                qn_t = jnp.tile(q_norm[jb], 2 * nha).astype(F32)
                sw_t = row(jnp.tile(subln[jb], nha))
                if seq_major:
                    x2 = _attn_prompt(x2.reshape(nb, t, d), row(norm_b[jb]), wq_t_b, qn_t.reshape(kd, 1),
                                      kv_ctx[2], kv_ctx[3], *lams, sw_t, w_o, jb, 512, 512, ahd,
                                      scale * LOG2E, lam_init).reshape(m, d)
                else:
                    q3 = _qproj(x2.reshape(ab, at, d), row(norm_b[jb]), w_q, jb,
                                qn_t.reshape(1, kd), gmat, tma, nha, scale)
                    o3 = yield ("attend", (q3.reshape(nb, t, kd), kv_ctx[0], kv_ctx[1], lams, lam_init))
                    x2 = _attn_out(o3.reshape(m, kd), x2, sw_t, w_o, jb, tm, nha, 1.0 - lam_init)
            if seq_major:
                x2 = yield ("mlp", (x2, row(norm_m[l]), l))
            else:
                x2 = _mlp(x2, row(norm_m[l]), w_up_b, w_down_b, l, min(1024, m), 1024)
        return x2.reshape(nb, t, d), states[0].reshape(n_a, nb, nh, hd, ds), states[1], k_new, v_new

    n_pool = cache_k.shape[0]
    cache_kt = jnp.transpose(cache_k, (0, 2, 3, 4, 1)).reshape(n_pool, kd, page)
    cache_v2 = cache_v.reshape(n_pool, page * nha, LANES)

    bp = x_prompt.shape[0]
    conv0_p = jnp.zeros((n_a, bp, CONV_W - 1, cdim), F32)
    ssm0_p = jnp.zeros((n_a, bp, nh, hd, ds), F32)
    prompt = trunk(x_prompt, conv0_p, ssm0_p, True)
    sample = trunk(x_sample, state_conv, state_ssm, False)

    m_p = bp * x_prompt.shape[1]
    nb_s, n_pages = page_table.shape
    share = (n_a + n_b) // n_b
    tm_f, tf_f, pps_f = 1024, 512, 8
    w_up_b, w_down_b = w_up.astype(BF16), w_down.astype(BF16)
    fused = ((n_a + n_b) % n_b == 0 and nb_s % share == 0 and m_p % tm_f == 0 and n_pages % pps_f == 0
             and (m_p // tm_f) * (w_up.shape[2] // tf_f) == (nb_s // share) * (n_pages // pps_f))

    def advance(gen, value):
        try:
            return gen.send(value), None
        except StopIteration as stop:
            return None, stop.value

    req_s, out_s = advance(sample, None)
    req_p, out_p = advance(prompt, None)
    while req_p is not None or req_s is not None:
        if fused and req_p is not None and req_s is not None:
            q3, kt_new, v_new, lams, lam_init = req_s[1]
            nseq = nb_s // share
            parts = []
            for k in range(share):
                x2, g_m, l = req_p[1]
                x2, o_part = _mlp_decode(x2, g_m, w_up_b, w_down_b, l, tm_f, tf_f, page_table, q3, kt_new, v_new,
                                         cache_kt, cache_v2, *lams, k * nseq, nseq, pps_f, ahd, lam_init)
                parts.append(o_part)
                req_p, out_p = advance(prompt, x2)
            req_s, out_s = advance(sample, jnp.concatenate(parts, axis=0))
        elif req_p is not None:
            x2, g_m, l = req_p[1]
            req_p, out_p = advance(prompt, _mlp(x2, g_m, w_up, w_down, l, min(1024, m_p), 1024))
        else:
            q3, kt_new, v_new, lams, lam_init = req_s[1]
            req_s, out_s = advance(sample, _attn_sample(page_table, q3, kt_new, v_new, cache_kt, cache_v2,
                                                        *lams, 16, ahd, lam_init))
    y_p, ssm_p, conv_p, k_p, v_p = out_p
    y_s, ssm_s, conv_s, k_s, v_s = out_s
    return (y_p, y_s, ssm_p, conv_p, k_p, v_p, ssm_s, conv_s, k_s, v_s)
```
